```python
import math
import jax, jax.numpy as jnp
from jax import lax
import numpy as np

D_MODEL = 1024
BATCH = 16
SEQ = 2048
DEPTH = 2

EPS = 1e-6
HEAD_DIM = 64
Q_BLOCK = 128
LRU_WIDTH = 512
LRU_BLOCKS = 8
LRU_BLOCK = LRU_WIDTH // LRU_BLOCKS
CONV_WIDTH = 4
LRU_C = 8.0
NSA_HEADS = 8
NSA_KV_GROUPS = 2
NSA_HPG = NSA_HEADS // NSA_KV_GROUPS
CMP_LEN = 32
CMP_STRIDE = 16
CMP_HIDDEN = 128
SLC_LEN = 64
SLC_TOPN = 8
WIN = 512
FORCE_SCORE = 1e4
DSA_HEADS = 16
DSA_KV_HEADS = 2
DSA_HPG = DSA_HEADS // DSA_KV_HEADS
IDX_HEADS = 8
IDX_DIM = 64
IDX_TOPK_MAX = 256
D_FF = -(-8 * D_MODEL // (3 * 256)) * 256
NSA_Q = NSA_HEADS * HEAD_DIM
NSA_KV = NSA_KV_GROUPS * HEAD_DIM
IN0_SIZES = (LRU_WIDTH, LRU_WIDTH, NSA_Q, NSA_KV, NSA_KV, NSA_KV, NSA_KV, NSA_KV, NSA_KV, NSA_HEADS * 3)
IN0 = sum(IN0_SIZES)
MIX0 = LRU_WIDTH + NSA_Q
IN1_SIZES = (DSA_HEADS * HEAD_DIM, DSA_KV_HEADS * HEAD_DIM, DSA_KV_HEADS * HEAD_DIM, IDX_HEADS * IDX_DIM, IDX_DIM, IDX_HEADS)
IN1 = sum(IN1_SIZES)
MIX1 = DSA_HEADS * HEAD_DIM

kernel_name = "hybrid_rglru_nsa_dsa_adaln"


def _split(t, sizes):
    return jnp.split(t, np.cumsum(sizes)[:-1].tolist(), axis=-1)


def rmsnorm(x, g):
    xf = x.astype(jnp.float32)
    y = xf * lax.rsqrt(jnp.mean(xf * xf, axis=-1, keepdims=True) + EPS)
    return (y * g.astype(jnp.float32)).astype(x.dtype)


def modulate(x, g, shift, scale):
    return rmsnorm(x, g) * (1.0 + scale[:, None, :]) + shift[:, None, :]


def masked_softmax(s, mask):
    s = jnp.where(mask, s.astype(jnp.float32), -jnp.inf)
    m = jnp.max(s, axis=-1, keepdims=True)
    m = jnp.where(jnp.isfinite(m), m, 0.0)
    p = jnp.exp(s - m)
    return p / jnp.maximum(jnp.sum(p, axis=-1, keepdims=True), 1e-30)


def swiglu(h, wg, wu, wd):
    return (jax.nn.silu(h @ wg) * (h @ wu)) @ wd


def adaln(c, w_mod, b_mod):
    mod = jax.nn.silu(c) @ w_mod + b_mod
    return jnp.split(mod, 6, axis=-1)


def causal_conv(x, w, b):
    S = x.shape[1]
    xp = jnp.pad(x, ((0, 0), (CONV_WIDTH - 1, 0), (0, 0)))
    out = xp[:, 0:S] * w[0]
    for k in range(1, CONV_WIDTH):
        out = out + xp[:, k:k + S] * w[k]
    return out + b


def rg_lru(x, wa, ba, wx, bx, lam):
    B, S, _ = x.shape
    xf = x.astype(jnp.float32)
    xb = xf.reshape(B, S, LRU_BLOCKS, LRU_BLOCK)
    r = jax.nn.sigmoid(jnp.einsum('bshi,hij->bshj', xb, wa).reshape(B, S, LRU_WIDTH) + ba)
    i = jax.nn.sigmoid(jnp.einsum('bshi,hij->bshj', xb, wx).reshape(B, S, LRU_WIDTH) + bx)
    log_a = -LRU_C * r * jax.nn.softplus(-lam.astype(jnp.float32))
    a = jnp.exp(log_a)
    mult = jnp.sqrt(-jnp.expm1(2.0 * log_a))
    mult = jnp.where((jnp.arange(S) == 0)[None, :, None], 1.0, mult)
    u = mult * (i * xf)

    def combine(e1, e2):
        a1, b1 = e1
        a2, b2 = e2
        return a1 * a2, a2 * b1 + b2

    _, h = lax.associative_scan(combine, (a, u), axis=1)
    return h.astype(x.dtype)


def compress_blocks(kv, pe, w1, w2):
    B, S, G, D = kv.shape
    nc = (S - CMP_LEN) // CMP_STRIDE + 1
    idx = jnp.arange(nc)[:, None] * CMP_STRIDE + jnp.arange(CMP_LEN)[None, :]
    blk = kv[:, idx] + pe[:, None, :]
    flat = blk.transpose(0, 1, 3, 2, 4).reshape(B, nc, G, CMP_LEN * D)
    return jax.nn.gelu(flat @ w1) @ w2


def nsa_mixer(q, k_cmp, v_cmp, k_slc, v_slc, k_win, v_win, gates, pe_k, w1_k, w2_k, pe_v, w1_v, w2_v):
    B, S = q.shape[:2]
    G, R, D = NSA_KV_GROUPS, NSA_HPG, HEAD_DIM
    scale = D ** -0.5
    nc = (S - CMP_LEN) // CMP_STRIDE + 1
    nsel = S // SLC_LEN
    n_top = min(SLC_TOPN, nsel)
    rs = lambda t: t.reshape(B, S, G, D)
    kc = compress_blocks(rs(k_cmp), pe_k, w1_k, w2_k)
    vc = compress_blocks(rs(v_cmp), pe_v, w1_v, w2_v)
    ks = rs(k_slc).reshape(B, nsel, SLC_LEN, G, D).transpose(0, 3, 1, 2, 4)
    vs = rs(v_slc).reshape(B, nsel, SLC_LEN, G, D).transpose(0, 3, 1, 2, 4)
    pad = ((0, 0), (WIN, 0), (0, 0), (0, 0))
    kw_pad = jnp.pad(rs(k_win), pad)
    vw_pad = jnp.pad(rs(v_win), pad)
    c_start = jnp.arange(nc) * CMP_STRIDE
    c_end = c_start + CMP_LEN - 1
    j_start = jnp.arange(nsel) * SLC_LEN
    overlap = ((c_start[:, None] < j_start[None, :] + SLC_LEN)
               & (c_start[:, None] + CMP_LEN > j_start[None, :])).astype(jnp.float32)
    jj = jnp.arange(nsel)
    b_i = jnp.arange(B)[:, None, None, None]
    g_i = jnp.arange(G)[None, :, None, None]

    def block(ib):
        t0 = ib * Q_BLOCK
        tq = t0 + jnp.arange(Q_BLOCK)
        qc = lax.dynamic_slice_in_dim(q, t0, Q_BLOCK, axis=1).reshape(B, Q_BLOCK, G, R, D)
        gc = lax.dynamic_slice_in_dim(gates, t0, Q_BLOCK, axis=1).reshape(B, Q_BLOCK, G, R, 3)
        s = jnp.einsum('bqgrd,bcgd->bgrqc', qc, kc) * scale
        p_cmp = masked_softmax(s, c_end[None, :] <= tq[:, None])
        o_cmp = jnp.einsum('bgrqc,bcgd->bqgrd', p_cmp, vc)
        imp = jnp.einsum('bgrqc,cj->bgqj', p_cmp, overlap)
        cur = tq // SLC_LEN
        forced = (jj[None, :] == 0) | (jj[None, :] == cur[:, None]) | (jj[None, :] == cur[:, None] - 1)
        imp = jnp.where(forced, FORCE_SCORE, imp)
        imp = jnp.where(jj[None, :] <= cur[:, None], imp, -jnp.inf)
        _, sel = lax.top_k(imp, n_top)
        kg = ks[b_i, g_i, sel].reshape(B, G, Q_BLOCK, n_top * SLC_LEN, D)
        vg = vs[b_i, g_i, sel].reshape(B, G, Q_BLOCK, n_top * SLC_LEN, D)
        kpos = sel[..., None] * SLC_LEN + jnp.arange(SLC_LEN)
        smask = (kpos <= tq[:, None, None]).reshape(B, G, 1, Q_BLOCK, n_top * SLC_LEN)
        s = jnp.einsum('bqgrd,bgqkd->bgrqk', qc, kg) * scale
        o_slc = jnp.einsum('bgrqk,bgqkd->bqgrd', masked_softmax(s, smask), vg)
        kw = lax.dynamic_slice_in_dim(kw_pad, t0, WIN + Q_BLOCK, axis=1)
        vw = lax.dynamic_slice_in_dim(vw_pad, t0, WIN + Q_BLOCK, axis=1)
        kpw = t0 - WIN + jnp.arange(WIN + Q_BLOCK)
        wmask = (kpw[None, :] >= 0) & (kpw[None, :] <= tq[:, None]) & (kpw[None, :] > tq[:, None] - WIN)
        s = jnp.einsum('bqgrd,bkgd->bgrqk', qc, kw) * scale
        o_win = jnp.einsum('bgrqk,bkgd->bqgrd', masked_softmax(s, wmask), vw)
        o = gc[..., 0:1] * o_cmp + gc[..., 1:2] * o_slc + gc[..., 2:3] * o_win
        return o.reshape(B, Q_BLOCK, G * R * D).astype(q.dtype)

    out = lax.map(block, jnp.arange(S // Q_BLOCK))
    return out.transpose(1, 0, 2, 3).reshape(B, S, G * R * D)


def dsa_mixer(q, k, v, q_idx, k_idx, w_idx):
    B, S = q.shape[:2]
    G, R, D = DSA_KV_HEADS, DSA_HPG, HEAD_DIM
    scale = D ** -0.5
    n_keep = min(IDX_TOPK_MAX, S // 4)
    kk = k.reshape(B, S, G, D)
    vv = v.reshape(B, S, G, D)
    qi_all = q_idx.reshape(B, S, IDX_HEADS, IDX_DIM)
    kpos = jnp.arange(S)
    b_i = jnp.arange(B)[:, None, None]

    def block(ib):
        t0 = ib * Q_BLOCK
        tq = t0 + jnp.arange(Q_BLOCK)
        qi = lax.dynamic_slice_in_dim(qi_all, t0, Q_BLOCK, axis=1)
        wi = lax.dynamic_slice_in_dim(w_idx, t0, Q_BLOCK, axis=1)
        sc = jax.nn.relu(jnp.einsum('bqhd,bsd->bqhs', qi, k_idx) * IDX_DIM ** -0.5)
        idx_score = jnp.einsum('bqhs,bqh->bqs', sc.astype(jnp.float32), wi.astype(jnp.float32)) * IDX_HEADS ** -0.5
        idx_score = jnp.where(kpos[None, :] <= tq[:, None], idx_score, -jnp.inf)
        _, sel = lax.top_k(idx_score, n_keep)
        kg = kk[b_i, sel]
        vg = vv[b_i, sel]
        qc = lax.dynamic_slice_in_dim(q, t0, Q_BLOCK, axis=1).reshape(B, Q_BLOCK, G, R, D)
        s = jnp.einsum('bqgrd,bqkgd->bgrqk', qc, kg) * scale
        mask = (sel <= tq[None, :, None])[:, None, None]
        o = jnp.einsum('bgrqk,bqkgd->bqgrd', masked_softmax(s, mask), vg)
        return o.reshape(B, Q_BLOCK, G * R * D).astype(q.dtype)

    out = lax.map(block, jnp.arange(S // Q_BLOCK))
    return out.transpose(1, 0, 2, 3).reshape(B, S, G * R * D)


def layer_ab(x, c, norm_mix, norm_ffn, mod_w, mod_b, w_in, conv_w, conv_b, wa, ba, wx, bx, lam,
             pe_k, w1_k, w2_k, pe_v, w1_v, w2_v, w_out, wg, wu, wd):
    sh_m, sc_m, g_m, sh_f, sc_f, g_f = adaln(c, mod_w, mod_b)
    h = modulate(x, norm_mix, sh_m, sc_m)
    (a_gate, a_x, q, kc, vc, ks, vs, kw, vw, gts) = _split(h @ w_in, IN0_SIZES)
    y_a = rg_lru(causal_conv(a_x, conv_w, conv_b), wa, ba, wx, bx, lam) * jax.nn.gelu(a_gate)
    y_b = nsa_mixer(q, kc, vc, ks, vs, kw, vw, jax.nn.sigmoid(gts), pe_k, w1_k, w2_k, pe_v, w1_v, w2_v)
    x = x + g_m[:, None, :] * (jnp.concatenate([y_a, y_b], axis=-1) @ w_out)
    h = modulate(x, norm_ffn, sh_f, sc_f)
    return x + g_f[:, None, :] * swiglu(h, wg, wu, wd)


def layer_c(x, c, norm_mix, norm_ffn, mod_w, mod_b, w_in, w_out, wg, wu, wd):
    sh_m, sc_m, g_m, sh_f, sc_f, g_f = adaln(c, mod_w, mod_b)
    h = modulate(x, norm_mix, sh_m, sc_m)
    q, k, v, q_idx, k_idx, w_idx = _split(h @ w_in, IN1_SIZES)
    x = x + g_m[:, None, :] * (dsa_mixer(q, k, v, q_idx, k_idx, w_idx) @ w_out)
    h = modulate(x, norm_ffn, sh_f, sc_f)
    return x + g_f[:, None, :] * swiglu(h, wg, wu, wd)


def setup_inputs(seed: int = 0) -> dict:
    key = jax.random.key(seed)
    keys = iter(jax.random.split(key, 64))

    def nrm(shape, s):
        return jax.random.normal(next(keys), shape, jnp.float32) * s

    def gain(n):
        return 1.0 + nrm((n,), 0.05)

    D = D_MODEL
    a0 = jax.random.uniform(next(keys), (LRU_WIDTH,), jnp.float32, minval=0.9, maxval=0.999)
    r0 = a0 ** (1.0 / LRU_C)
    lam = jnp.log(r0) - jnp.log1p(-r0)
    return {
        "x": nrm((BATCH, SEQ, D), 1.0),
        "c": nrm((BATCH, D), 1.0),
        "l0_norm_mix": gain(D),
        "l0_norm_ffn": gain(D),
        "l0_mod_w": nrm((D, 6 * D), 0.5 * D ** -0.5),
        "l0_mod_b": nrm((6 * D,), 0.01),
        "l0_w_in": nrm((D, IN0), D ** -0.5),
        "l0_conv_w": nrm((CONV_WIDTH, LRU_WIDTH), CONV_WIDTH ** -0.5),
        "l0_conv_b": nrm((LRU_WIDTH,), 0.01),
        "l0_lru_wa": nrm((LRU_BLOCKS, LRU_BLOCK, LRU_BLOCK), LRU_BLOCK ** -0.5),
        "l0_lru_ba": nrm((LRU_WIDTH,), 0.01),
        "l0_lru_wx": nrm((LRU_BLOCKS, LRU_BLOCK, LRU_BLOCK), LRU_BLOCK ** -0.5),
        "l0_lru_bx": nrm((LRU_WIDTH,), 0.01),
        "l0_lru_lambda": lam,
        "l0_cmp_pe_k": nrm((CMP_LEN, HEAD_DIM), 0.5),
        "l0_cmp_w1_k": nrm((CMP_LEN * HEAD_DIM, CMP_HIDDEN), (CMP_LEN * HEAD_DIM) ** -0.5),
        "l0_cmp_w2_k": nrm((CMP_HIDDEN, HEAD_DIM), CMP_HIDDEN ** -0.5),
        "l0_cmp_pe_v": nrm((CMP_LEN, HEAD_DIM), 0.5),
        "l0_cmp_w1_v": nrm((CMP_LEN * HEAD_DIM, CMP_HIDDEN), (CMP_LEN * HEAD_DIM) ** -0.5),
        "l0_cmp_w2_v": nrm((CMP_HIDDEN, HEAD_DIM), CMP_HIDDEN ** -0.5),
        "l0_w_out": nrm((MIX0, D), MIX0 ** -0.5),
        "l0_ffn_wg": nrm((D, D_FF), D ** -0.5),
        "l0_ffn_wu": nrm((D, D_FF), D ** -0.5),
        "l0_ffn_wd": nrm((D_FF, D), D_FF ** -0.5),
        "l1_norm_mix": gain(D),
        "l1_norm_ffn": gain(D),
        "l1_mod_w": nrm((D, 6 * D), 0.5 * D ** -0.5),
        "l1_mod_b": nrm((6 * D,), 0.01),
        "l1_w_in": nrm((D, IN1), D ** -0.5),
        "l1_w_out": nrm((MIX1, D), MIX1 ** -0.5),
        "l1_ffn_wg": nrm((D, D_FF), D ** -0.5),
        "l1_ffn_wu": nrm((D, D_FF), D ** -0.5),
        "l1_ffn_wd": nrm((D_FF, D), D_FF ** -0.5),
        "final_norm": gain(D),
    }


def reference(x, c,
              l0_norm_mix, l0_norm_ffn, l0_mod_w, l0_mod_b, l0_w_in, l0_conv_w, l0_conv_b,
              l0_lru_wa, l0_lru_ba, l0_lru_wx, l0_lru_bx, l0_lru_lambda,
              l0_cmp_pe_k, l0_cmp_w1_k, l0_cmp_w2_k, l0_cmp_pe_v, l0_cmp_w1_v, l0_cmp_w2_v,
              l0_w_out, l0_ffn_wg, l0_ffn_wu, l0_ffn_wd,
              l1_norm_mix, l1_norm_ffn, l1_mod_w, l1_mod_b, l1_w_in, l1_w_out,
              l1_ffn_wg, l1_ffn_wu, l1_ffn_wd,
              final_norm):
    layer_params = [
        (l0_norm_mix, l0_norm_ffn, l0_mod_w, l0_mod_b, l0_w_in, l0_conv_w, l0_conv_b,
         l0_lru_wa, l0_lru_ba, l0_lru_wx, l0_lru_bx, l0_lru_lambda,
         l0_cmp_pe_k, l0_cmp_w1_k, l0_cmp_w2_k, l0_cmp_pe_v, l0_cmp_w1_v, l0_cmp_w2_v,
         l0_w_out, l0_ffn_wg, l0_ffn_wu, l0_ffn_wd),
        (l1_norm_mix, l1_norm_ffn, l1_mod_w, l1_mod_b, l1_w_in, l1_w_out,
         l1_ffn_wg, l1_ffn_wu, l1_ffn_wd),
    ]
    for layer in range(DEPTH):
        if layer % 2 == 0:
            x = layer_ab(x, c, *layer_params[layer])
        else:
            x = layer_c(x, c, *layer_params[layer])
    return rmsnorm(x, final_norm)
```

```python
import functools

import numpy as np
import jax
import jax.numpy as jnp
from jax import lax
from jax.experimental import pallas as pl
from jax.experimental.pallas import tpu as pltpu

F32 = jnp.float32
BF16 = jnp.bfloat16

EPS = 1e-6
HEAD_DIM = 64
Q_BLOCK = 128
LRU_WIDTH = 512
LRU_BLOCKS = 8
CONV_WIDTH = 4
LRU_C = 8.0
NSA_HEADS = 8
NSA_KV_GROUPS = 2
NSA_HPG = NSA_HEADS // NSA_KV_GROUPS
CMP_LEN = 32
CMP_STRIDE = 16
CMP_HIDDEN = 128
SLC_LEN = 64
SLC_TOPN = 8
WIN = 512
FORCE_SCORE = 1e4
DSA_HEADS = 16
DSA_KV_HEADS = 2
DSA_HPG = DSA_HEADS // DSA_KV_HEADS
IDX_HEADS = 8
IDX_DIM = 64
IDX_TOPK_MAX = 256

LANES = 128
NEG = -1e30
VMEM_LIMIT = 56 * 1024 * 1024
INT_MIN = -2147483648
KEY_NEG_INF = int(np.array(0xFF800000 ^ 0x7FFFFFFF, np.uint32).astype(np.int32))


def _cparams(n_grid):
    return pltpu.CompilerParams(dimension_semantics=("arbitrary",) * n_grid,
                                vmem_limit_bytes=VMEM_LIMIT)


def _const_spec(shape):
    nd = len(shape)
    return pl.BlockSpec(shape, lambda *_: (0,) * nd, pipeline_mode=pl.Buffered(1))


def _dot(a, b):
    return jnp.dot(a, b, preferred_element_type=F32)


def _dot_nt(a, b):
    return lax.dot_general(a, b, (((1,), (1,)), ((), ())), preferred_element_type=F32)


def _rms(x, g):
    return x * lax.rsqrt(jnp.mean(x * x, axis=-1, keepdims=True) + EPS) * g


def _masked_softmax_pv(s, v):
    m = jnp.max(s, axis=-1, keepdims=True)
    p = jnp.exp(s - m)
    l = jnp.sum(p, axis=-1, keepdims=True)
    inv = jnp.where(m > 0.5 * NEG, 1.0 / jnp.maximum(l, 1e-30), 0.0)
    o = _dot(p.astype(BF16), v) * inv
    return o, p, inv


def _adaln_kernel(c_ref, w_ref, b_ref, o_ref):
    c = c_ref[...]
    a = (c * jax.nn.sigmoid(c)).astype(BF16)
    o_ref[...] = _dot(a, w_ref[...].astype(BF16)) + b_ref[...]


def _adaln(c, w, b):
    B, D = c.shape
    N = w.shape[1]
    tn = N // 4
    return pl.pallas_call(
        _adaln_kernel,
        grid=(N // tn,),
        in_specs=[pl.BlockSpec((B, D), lambda j: (0, 0)),
                  pl.BlockSpec((D, tn), lambda j: (0, j)),
                  pl.BlockSpec((1, tn), lambda j: (0, j))],
        out_specs=pl.BlockSpec((B, tn), lambda j: (0, j)),
        out_shape=jax.ShapeDtypeStruct((B, N), F32),
        compiler_params=_cparams(1),
        name="adaln",
    )(c, w, b.reshape(1, N))


def _inproj_kernel(x_ref, mod_ref, g_ref, w_ref, *o_refs, splits):
    x = x_ref[0]
    h = _rms(x, g_ref[...]) * (1.0 + mod_ref[0, 1:2, :]) + mod_ref[0, 0:1, :]
    acc = _dot(h.astype(BF16), w_ref[...])
    for o_ref, (c0, wd) in zip(o_refs, splits):
        o_ref[0] = acc[:, c0:c0 + wd].astype(o_ref.dtype)


def _inproj(x, mod, g, w, outs, tm=512):
    B, S, D = x.shape
    N = w.shape[1]
    splits, c0 = [], 0
    for wd, _ in outs:
        splits.append((c0, wd))
        c0 += wd
    assert c0 == N
    return pl.pallas_call(
        functools.partial(_inproj_kernel, splits=tuple(splits)),
        grid=(B, S // tm),
        in_specs=[pl.BlockSpec((1, tm, D), lambda b, i: (b, i, 0)),
                  pl.BlockSpec((1, 6, D), lambda b, i: (b, 0, 0)),
                  _const_spec((1, D)),
                  _const_spec((D, N))],
        out_specs=[pl.BlockSpec((1, tm, wd), lambda b, i: (b, i, 0)) for wd, _ in outs],
        out_shape=[jax.ShapeDtypeStruct((B, S, wd), dt) for wd, dt in outs],
        compiler_params=_cparams(2),
        name="inproj",
    )(x, mod, g.reshape(1, D), w)


def _lru_kernel(ag_ref, ax_ref, cw_ref, cb_ref, wa_ref, ba_ref, wx_ref, bx_ref, lam_ref, o_ref,
                xpad, a_s, u_s, *, S, C, TC):
    PAD = 8
    xpad[0:PAD, :] = jnp.zeros((PAD, C), F32)
    xpad[PAD:PAD + S, :] = ax_ref[0]
    z = -lam_ref[...]
    sp = jnp.maximum(z, 0.0) + jnp.log(1.0 + jnp.exp(-jnp.abs(z)))
    for ci in range(S // TC):
        r0 = ci * TC
        xc = cb_ref[...]
        for k in range(CONV_WIDTH):
            off = PAD - (CONV_WIDTH - 1) + k + r0
            xc = xc + xpad[off:off + TC, :] * cw_ref[k:k + 1, :]
        xb = xc.astype(BF16)
        r = jax.nn.sigmoid(_dot(xb, wa_ref[...]) + ba_ref[...])
        i = jax.nn.sigmoid(_dot(xb, wx_ref[...]) + bx_ref[...])
        log_a = -LRU_C * r * sp
        a = jnp.exp(log_a)
        mult = jnp.sqrt(1.0 - jnp.exp(2.0 * log_a))
        if ci == 0:
            row = lax.broadcasted_iota(jnp.int32, (TC, C), 0)
            mult = jnp.where(row == 0, 1.0, mult)
        a_s[r0:r0 + TC, :] = a
        u_s[r0:r0 + TC, :] = mult * (i * xc)

    row8 = lax.broadcasted_iota(jnp.int32, (8, C), 0)

    def tile(ti, h):
        t0 = pl.multiple_of(ti * 8, 8)
        A = a_s[pl.ds(t0, 8), :]
        U = u_s[pl.ds(t0, 8), :]
        for d in (1, 2, 4):
            As = pltpu.roll(A, d, 0)
            Us = pltpu.roll(U, d, 0)
            ok = row8 >= d
            U = jnp.where(ok, A * Us + U, U)
            A = jnp.where(ok, A * As, A)
        H = U + A * h
        u_s[pl.ds(t0, 8), :] = H
        return H[7:8, :]

    lax.fori_loop(0, S // 8, tile, jnp.zeros((1, C), F32))

    for ci in range(S // TC):
        r0 = ci * TC
        o_ref[0, r0:r0 + TC, :] = (u_s[r0:r0 + TC, :]
                                   * jax.nn.gelu(ag_ref[0, r0:r0 + TC, :])).astype(o_ref.dtype)


def _lru(ag, ax, cw, cb, wa_bd, ba, wx_bd, bx, lam):
    B, S, C = ag.shape
    TC = 256
    row = lambda v: v.reshape(1, C)
    return pl.pallas_call(
        functools.partial(_lru_kernel, S=S, C=C, TC=TC),
        grid=(B,),
        in_specs=[pl.BlockSpec((1, S, C), lambda b: (b, 0, 0)),
                  pl.BlockSpec((1, S, C), lambda b: (b, 0, 0)),
                  _const_spec((CONV_WIDTH, C)), _const_spec((1, C)),
                  _const_spec((C, C)), _const_spec((1, C)),
                  _const_spec((C, C)), _const_spec((1, C)), _const_spec((1, C))],
        out_specs=pl.BlockSpec((1, S, C), lambda b: (b, 0, 0)),
        out_shape=jax.ShapeDtypeStruct((B, S, C), BF16),
        scratch_shapes=[pltpu.VMEM((S + 8, C), F32), pltpu.VMEM((S, C), F32), pltpu.VMEM((S, C), F32)],
        compiler_params=_cparams(1),
        name="rglru",
    )(ag, ax, cw, row(cb), wa_bd, row(ba), wx_bd, row(bx), row(lam))


def _compress_kernel(xk_ref, xv_ref, pek_ref, w1k_ref, w2k_ref, pev_ref, w1v_ref, w2v_ref,
                     ok_ref, ov_ref):
    def one(x_ref, pe_ref, w1_ref, w2_ref, o_ref):
        x = x_ref[0]
        p0 = _dot((x + pe_ref[0:1, :]).astype(BF16), w1_ref[0])
        p1 = _dot((x + pe_ref[1:2, :]).astype(BF16), w1_ref[1])
        nc = x.shape[0]
        hid = p0 + pltpu.roll(p1, nc - 1, 0)
        o_ref[0] = _dot(jax.nn.gelu(hid).astype(BF16), w2_ref[...]).astype(o_ref.dtype)

    one(xk_ref, pek_ref, w1k_ref, w2k_ref, ok_ref)
    one(xv_ref, pev_ref, w1v_ref, w2v_ref, ov_ref)


def _compress(xk, xv, pek, w1k, w2k, pev, w1v, w2v):
    B, NC, W = xk.shape
    GH = NSA_KV_GROUPS * CMP_HIDDEN
    GD = NSA_KV_GROUPS * HEAD_DIM
    xspec = pl.BlockSpec((1, NC, W), lambda b: (b, 0, 0))
    ospec = pl.BlockSpec((1, NC, GD), lambda b: (b, 0, 0))
    return pl.pallas_call(
        _compress_kernel,
        grid=(B,),
        in_specs=[xspec, xspec,
                  _const_spec((2, W)), _const_spec((2, W, GH)), _const_spec((GH, GD)),
                  _const_spec((2, W)), _const_spec((2, W, GH)), _const_spec((GH, GD))],
        out_specs=[ospec, ospec],
        out_shape=[jax.ShapeDtypeStruct((B, NC, GD), BF16)] * 2,
        compiler_params=_cparams(1),
        name="nsa_compress",
    )(xk, xv, pek, w1k, w2k, pev, w1v, w2v)


def _nsa_kernel(q_ref, gt_ref, kc_ref, vc_ref, kv_ref, ovt_ref, exp_ref, o_ref, *, S, QB, NC, NSEL, NTOP):
    ib = pl.program_id(1)
    t0 = ib * QB
    R = NSA_HPG
    low = lax.broadcasted_iota(jnp.int32, (QB, LANES), 1) < HEAD_DIM
    q = q_ref[0] * jnp.asarray(HEAD_DIM ** -0.5, BF16)
    gates = jax.nn.sigmoid(gt_ref[0])
    tq_col = t0 + lax.broadcasted_iota(jnp.int32, (QB, 1), 0)

    def qstack(g):
        keep = low if g == 0 else jnp.logical_not(low)
        return jnp.concatenate(
            [jnp.where(keep, q[:, r * LANES:(r + 1) * LANES], jnp.zeros((), BF16)) for r in range(R)], axis=0)

    def rep(b):
        return jnp.concatenate([b] * R, axis=0)

    cpos = lax.broadcasted_iota(jnp.int32, (QB, NC), 1) * CMP_STRIDE + (CMP_LEN - 1)
    bias_c = rep(jnp.where(cpos <= tq_col, 0.0, NEG))
    start = pl.multiple_of(jnp.maximum(t0 - WIN, 0), QB)
    kpos_w = start + lax.broadcasted_iota(jnp.int32, (QB, WIN + QB), 1)
    bias_w = rep(jnp.where((kpos_w <= tq_col) & (kpos_w > tq_col - WIN), 0.0, NEG))
    kwin = kv_ref[0, pl.ds(start, WIN + QB), 2 * LANES:3 * LANES]
    vwin = kv_ref[0, pl.ds(start, WIN + QB), 3 * LANES:4 * LANES]
    kslc = kv_ref[0, :, 0:LANES]
    vslc = kv_ref[0, :, LANES:2 * LANES]
    kpos_s = lax.broadcasted_iota(jnp.int32, (QB, S), 1)
    causal_s = kpos_s <= tq_col
    jrow = lax.broadcasted_iota(jnp.int32, (LANES, QB), 0)
    cur = lax.shift_right_logical(t0 + lax.broadcasted_iota(jnp.int32, (LANES, QB), 1), 6)
    forced = (jrow == 0) | (jrow == cur) | (jrow == cur - 1)

    o_cmp, o_slc, o_win = [], [], []
    for g in range(NSA_KV_GROUPS):
        qg = qstack(g)
        oc, p, inv = _masked_softmax_pv(_dot_nt(qg, kc_ref[0]) + bias_c, vc_ref[0])
        o_cmp.append(oc)
        pn = p * inv
        psum = pn[0:QB]
        for r in range(1, R):
            psum = psum + pn[r * QB:(r + 1) * QB]
        hi = psum.astype(BF16)
        lo = (psum - hi.astype(F32)).astype(BF16)
        imp = _dot_nt(ovt_ref[...], hi) + _dot_nt(ovt_ref[...], lo)
        imp = jnp.where(forced, FORCE_SCORE, imp)
        imp = jnp.where(jrow <= cur, imp, -jnp.inf)
        imp = imp[0:NSEL]
        jr = jrow[0:NSEL]
        cnt = jnp.zeros((NSEL, QB), F32)
        for j2 in range(NSEL):
            rowv = imp[j2:j2 + 1, :]
            cnt = cnt + jnp.where(jr > j2, jnp.where(rowv >= imp, 1.0, 0.0), jnp.where(rowv > imp, 1.0, 0.0))
        sel_t = jnp.where(cnt < NTOP, 1.0, 0.0)
        if NSEL < LANES:
            sel_t = jnp.concatenate([sel_t, jnp.zeros((LANES - NSEL, QB), F32)], axis=0)
        sel = sel_t.T.astype(BF16)
        picked = _dot(sel, exp_ref[...])
        bias_s = rep(jnp.where((picked > 0.5) & causal_s, 0.0, NEG))
        os_, _, _ = _masked_softmax_pv(_dot_nt(qg, kslc) + bias_s, vslc)
        o_slc.append(os_)
        ow, _, _ = _masked_softmax_pv(_dot_nt(qg, kwin) + bias_w, vwin)
        o_win.append(ow)

    for r in range(R):
        acc = jnp.zeros((QB, LANES), F32)
        for j, branch in enumerate((o_cmp, o_slc, o_win)):
            val = jnp.where(low, branch[0][r * QB:(r + 1) * QB], branch[1][r * QB:(r + 1) * QB])
            ca, cb = 3 * r + j, 3 * (R + r) + j
            gate = jnp.where(low, gates[:, ca:ca + 1], gates[:, cb:cb + 1])
            acc = acc + gate * val
        o_ref[0, :, r * LANES:(r + 1) * LANES] = acc.astype(o_ref.dtype)


def _nsa(q, gts, kc, vc, kv4):
    B, S, _ = q.shape
    QB = Q_BLOCK
    NC = kc.shape[1]
    NSEL = S // SLC_LEN
    NTOP = min(SLC_TOPN, NSEL)
    assert NSEL <= LANES and S % QB == 0 and S >= WIN + QB
    c = np.arange(NC)[None, :] * CMP_STRIDE
    j = np.arange(LANES)[:, None] * SLC_LEN
    valid_c = np.arange(NC)[None, :] < (S - CMP_LEN) // CMP_STRIDE + 1
    ovt = ((c < j + SLC_LEN) & (c + CMP_LEN > j) & valid_c & (np.arange(LANES)[:, None] < NSEL))
    expand = (np.arange(S)[None, :] // SLC_LEN) == np.arange(LANES)[:, None]
    GD = NSA_KV_GROUPS * HEAD_DIM
    return pl.pallas_call(
        functools.partial(_nsa_kernel, S=S, QB=QB, NC=NC, NSEL=NSEL, NTOP=NTOP),
        grid=(B, S // QB),
        in_specs=[pl.BlockSpec((1, QB, NSA_HPG * LANES), lambda b, i: (b, i, 0)),
                  pl.BlockSpec((1, QB, LANES), lambda b, i: (b, i, 0)),
                  pl.BlockSpec((1, NC, GD), lambda b, i: (b, 0, 0)),
                  pl.BlockSpec((1, NC, GD), lambda b, i: (b, 0, 0)),
                  pl.BlockSpec((1, S, 4 * GD), lambda b, i: (b, 0, 0)),
                  _const_spec((LANES, NC)), _const_spec((LANES, S))],
        out_specs=pl.BlockSpec((1, QB, NSA_HPG * LANES), lambda b, i: (b, i, 0)),
        out_shape=jax.ShapeDtypeStruct((B, S, NSA_HPG * LANES), BF16),
        compiler_params=_cparams(2),
        name="nsa_attention",
    )(q, gts, kc, vc, kv4, jnp.asarray(ovt, BF16), jnp.asarray(expand, BF16))


def _dsa_kernel(q_ref, k_ref, v_ref, qi_ref, ki_ref, wi_ref, o_ref, *, S, QB, NKEEP, CH):
    ib = pl.program_id(1)
    t0 = ib * QB
    low = lax.broadcasted_iota(jnp.int32, (QB, LANES), 1) < HEAD_DIM
    not_low = jnp.logical_not(low)
    zero = jnp.zeros((), BF16)
    tq_col = t0 + lax.broadcasted_iota(jnp.int32, (QB, 1), 0)
    causal = lax.broadcasted_iota(jnp.int32, (QB, S), 1) <= tq_col

    w = wi_ref[0] * (IDX_DIM ** -0.5 * IDX_HEADS ** -0.5)
    qi = qi_ref[0]
    ki = ki_ref[0]
    score = None
    for h in range(IDX_HEADS):
        slot = qi[:, (h // 2) * LANES:(h // 2 + 1) * LANES]
        qh = jnp.where(low if h % 2 == 0 else not_low, slot, zero)
        term = jnp.maximum(_dot_nt(qh, ki), 0.0) * w[:, h:h + 1]
        score = term if score is None else score + term

    bits = pltpu.bitcast(score, jnp.int32)
    key = bits ^ (lax.shift_right_arithmetic(bits, 31) & 0x7FFFFFFF)
    key = jnp.where(causal, key, KEY_NEG_INF)

    def bit_step(i, t_u):
        c_u = t_u | lax.shift_left(jnp.int32(1), 31 - i)
        cnt = jnp.sum(jnp.where(key >= (c_u ^ INT_MIN), 1.0, 0.0), axis=-1, keepdims=True)
        return jnp.where(cnt >= NKEEP, c_u, t_u)

    thr = lax.fori_loop(0, 32, bit_step, jnp.zeros((QB, 1), jnp.int32)) ^ INT_MIN
    gt = key > thr
    eq = key == thr
    need = NKEEP - jnp.sum(jnp.where(gt, 1.0, 0.0), axis=-1, keepdims=True)
    eq_b = jnp.where(eq, 1.0, 0.0).astype(BF16)
    tri = (lax.broadcasted_iota(jnp.int32, (CH, CH), 0)
           < lax.broadcasted_iota(jnp.int32, (CH, CH), 1)).astype(BF16)
    ranks, before = [], jnp.zeros((QB, 1), F32)
    for c in range(S // CH):
        e = eq_b[:, c * CH:(c + 1) * CH]
        ranks.append(_dot(e, tri) + before)
        before = before + jnp.sum(e.astype(F32), axis=-1, keepdims=True)
    rank = jnp.concatenate(ranks, axis=1)
    keep = (gt | (eq & (rank < need))) & causal
    bias = jnp.where(keep, 0.0, NEG)
    bias2 = jnp.concatenate([bias, bias], axis=0)

    k = k_ref[0]
    v = v_ref[0]
    scale = jnp.asarray(HEAD_DIM ** -0.5, BF16)
    for r in range(DSA_HPG):
        slot = q_ref[0, :, r * LANES:(r + 1) * LANES] * scale
        q2 = jnp.concatenate([jnp.where(low, slot, zero), jnp.where(not_low, slot, zero)], axis=0)
        o, _, _ = _masked_softmax_pv(_dot_nt(q2, k) + bias2, v)
        o_ref[0, :, r * LANES:(r + 1) * LANES] = jnp.where(low, o[0:QB], o[QB:2 * QB]).astype(o_ref.dtype)


def _dsa(q, k, v, qi, ki, wi):
    B, S, _ = q.shape
    QB = Q_BLOCK
    NKEEP = min(IDX_TOPK_MAX, S // 4)
    GD = DSA_KV_HEADS * HEAD_DIM
    blk = lambda w: pl.BlockSpec((1, QB, w), lambda b, i: (b, i, 0))
    full = lambda w: pl.BlockSpec((1, S, w), lambda b, i: (b, 0, 0))
    return pl.pallas_call(
        functools.partial(_dsa_kernel, S=S, QB=QB, NKEEP=NKEEP, CH=256),
        grid=(B, S // QB),
        in_specs=[blk(DSA_HPG * LANES), full(GD), full(GD), blk(IDX_HEADS * IDX_DIM), full(LANES), blk(LANES)],
        out_specs=blk(DSA_HPG * LANES),
        out_shape=jax.ShapeDtypeStruct((B, S, DSA_HPG * LANES), BF16),
        compiler_params=_cparams(2),
        name="dsa_attention",
    )(q, k, v, qi, ki, wi)


def _mix_ffn_kernel(*refs, n_y, final):
    x_ref, mod_ref, gf_ref = refs[0:3]
    y_refs = refs[3:3 + n_y]
    wo_refs = refs[3 + n_y:3 + 2 * n_y]
    wg_ref, wu_ref, wd_ref = refs[3 + 2 * n_y:6 + 2 * n_y]
    rest = refs[6 + 2 * n_y:]
    o_ref = rest[-1]
    mix = None
    for y_ref, wo_ref in zip(y_refs, wo_refs):
        t = _dot(y_ref[0], wo_ref[...])
        mix = t if mix is None else mix + t
    x1 = x_ref[0] + mod_ref[0, 2:3, :] * mix
    h = (_rms(x1, gf_ref[...]) * (1.0 + mod_ref[0, 4:5, :]) + mod_ref[0, 3:4, :]).astype(BF16)
    gate = _dot(h, wg_ref[...])
    up = _dot(h, wu_ref[...])
    act = (gate * jax.nn.sigmoid(gate) * up).astype(BF16)
    x2 = x1 + mod_ref[0, 5:6, :] * _dot(act, wd_ref[...])
    if final:
        x2 = _rms(x2, rest[0][...])
    o_ref[0] = x2


def _mix_ffn(x, mod, g_ffn, ys, wos, wg, wu, wd, g_final=None, tm=512):
    B, S, D = x.shape
    FF = wg.shape[1]
    final = g_final is not None
    row_blk = lambda w: pl.BlockSpec((1, tm, w), lambda b, i: (b, i, 0))
    in_specs = ([row_blk(D), pl.BlockSpec((1, 6, D), lambda b, i: (b, 0, 0)), _const_spec((1, D))]
                + [row_blk(y.shape[2]) for y in ys]
                + [_const_spec(w.shape) for w in wos]
                + [_const_spec((D, FF)), _const_spec((D, FF)), _const_spec((FF, D))])
    args = [x, mod, g_ffn.reshape(1, D), *ys, *wos, wg, wu, wd]
    if final:
        in_specs.append(_const_spec((1, D)))
        args.append(g_final.reshape(1, D))
    return pl.pallas_call(
        functools.partial(_mix_ffn_kernel, n_y=len(ys), final=final),
        grid=(B, S // tm),
        in_specs=in_specs,
        out_specs=row_blk(D),
        out_shape=jax.ShapeDtypeStruct((B, S, D), F32),
        compiler_params=_cparams(2),
        name="mix_ffn",
    )(*args)


def _block_diag(w):
    nb, bi, bo = w.shape
    out = jnp.zeros((nb * bi, nb * bo), w.dtype)
    for h in range(nb):
        out = out.at[h * bi:(h + 1) * bi, h * bo:(h + 1) * bo].set(w[h])
    return out


def _pair_heads(n_groups, per_group):
    assert n_groups == 2
    return [g * per_group + r for r in range(per_group) for g in range(n_groups)]


def _head_cols(order):
    return np.concatenate([np.arange(h * HEAD_DIM, (h + 1) * HEAD_DIM) for h in order])


def _pad_cols(w, n):
    return jnp.pad(w, ((0, 0), (0, n - w.shape[1])))


def _compress_weights(pe, w1, w2):
    G, D, H = NSA_KV_GROUPS, HEAD_DIM, CMP_HIDDEN
    half = CMP_LEN // 2
    pe_r = jnp.broadcast_to(pe.reshape(2, half, 1, D), (2, half, G, D)).reshape(2, half * G * D)
    w1r = w1.reshape(2, half, D, H)
    w1e = jnp.zeros((2, half, G, D, G, H), w1.dtype)
    for g in range(G):
        w1e = w1e.at[:, :, g, :, g, :].set(w1r)
    w1e = w1e.reshape(2, half * G * D, G * H)
    w2e = jnp.zeros((G, H, G, D), w2.dtype)
    for g in range(G):
        w2e = w2e.at[g, :, g, :].set(w2)
    return pe_r, w1e.astype(BF16), w2e.reshape(G * H, G * D).astype(BF16)


def _layer_ab(x, c, norm_mix, norm_ffn, mod_w, mod_b, w_in, conv_w, conv_b, wa, ba, wx, bx, lam,
              pe_k, w1_k, w2_k, pe_v, w1_v, w2_v, w_out, wg, wu, wd):
    B, S, D = x.shape
    mod = _adaln(c, mod_w, mod_b).reshape(B, 6, D)
    C = LRU_WIDTH
    HQ = NSA_HEADS * HEAD_DIM
    GD = NSA_KV_GROUPS * HEAD_DIM
    order = _pair_heads(NSA_KV_GROUPS, NSA_HPG)
    o_q = 2 * C
    o_kv = o_q + HQ
    o_gt = o_kv + 6 * GD
    w_q = w_in[:, o_q:o_kv][:, _head_cols(order)]
    w_all = jnp.concatenate([w_in[:, :o_q], w_q, w_in[:, o_kv:o_gt], _pad_cols(w_in[:, o_gt:], LANES)],
                            axis=1).astype(BF16)
    ag, ax, q, kcmp, vcmp, kv4, gts = _inproj(
        x, mod, norm_mix, w_all,
        [(C, F32), (C, F32), (HQ, BF16), (GD, F32), (GD, F32), (4 * GD, BF16), (LANES, F32)])
    y_a = _lru(ag, ax, conv_w, conv_b, _block_diag(wa).astype(BF16), ba, _block_diag(wx).astype(BF16), bx, lam)
    NC = S // CMP_STRIDE
    kc, vc = _compress(kcmp.reshape(B, NC, CMP_STRIDE * GD), vcmp.reshape(B, NC, CMP_STRIDE * GD),
                       *_compress_weights(pe_k, w1_k, w2_k), *_compress_weights(pe_v, w1_v, w2_v))
    y_b = _nsa(q, gts, kc, vc, kv4)
    wo_a = w_out[:C].astype(BF16)
    wo_b = w_out[C:][_head_cols(order)].astype(BF16)
    return _mix_ffn(x, mod, norm_ffn, [y_a, y_b], [wo_a, wo_b],
                    wg.astype(BF16), wu.astype(BF16), wd.astype(BF16))


def _layer_c(x, c, norm_mix, norm_ffn, mod_w, mod_b, w_in, w_out, wg, wu, wd, final_norm):
    B, S, D = x.shape
    mod = _adaln(c, mod_w, mod_b).reshape(B, 6, D)
    HQ = DSA_HEADS * HEAD_DIM
    GD = DSA_KV_HEADS * HEAD_DIM
    HI = IDX_HEADS * IDX_DIM
    order = _pair_heads(DSA_KV_HEADS, DSA_HPG)
    o_k = HQ
    o_v = o_k + GD
    o_qi = o_v + GD
    o_ki = o_qi + HI
    o_wi = o_ki + IDX_DIM
    w_ki = w_in[:, o_ki:o_wi]
    w_all = jnp.concatenate([w_in[:, :HQ][:, _head_cols(order)], w_in[:, o_k:o_qi], w_in[:, o_qi:o_ki],
                             w_ki, w_ki, _pad_cols(w_in[:, o_wi:], LANES)], axis=1).astype(BF16)
    q, k, v, qi, ki, wi = _inproj(
        x, mod, norm_mix, w_all,
        [(HQ, BF16), (GD, BF16), (GD, BF16), (HI, BF16), (LANES, BF16), (LANES, F32)])
    y = _dsa(q, k, v, qi, ki, wi)
    wo = w_out[_head_cols(order)].astype(BF16)
    return _mix_ffn(x, mod, norm_ffn, [y], [wo], wg.astype(BF16), wu.astype(BF16), wd.astype(BF16),
                    g_final=final_norm)


def kernel(x, c, l0_norm_mix, l0_norm_ffn, l0_mod_w, l0_mod_b, l0_w_in, l0_conv_w, l0_conv_b, l0_lru_wa, l0_lru_ba, l0_lru_wx, l0_lru_bx, l0_lru_lambda, l0_cmp_pe_k, l0_cmp_w1_k, l0_cmp_w2_k, l0_cmp_pe_v, l0_cmp_w1_v, l0_cmp_w2_v, l0_w_out, l0_ffn_wg, l0_ffn_wu, l0_ffn_wd, l1_norm_mix, l1_norm_ffn, l1_mod_w, l1_mod_b, l1_w_in, l1_w_out, l1_ffn_wg, l1_ffn_wu, l1_ffn_wd, final_norm):
    x = _layer_ab(x, c, l0_norm_mix, l0_norm_ffn, l0_mod_w, l0_mod_b, l0_w_in, l0_conv_w, l0_conv_b,
                  l0_lru_wa, l0_lru_ba, l0_lru_wx, l0_lru_bx, l0_lru_lambda,
                  l0_cmp_pe_k, l0_cmp_w1_k, l0_cmp_w2_k, l0_cmp_pe_v, l0_cmp_w1_v, l0_cmp_w2_v,
                  l0_w_out, l0_ffn_wg, l0_ffn_wu, l0_ffn_wd)
    return _layer_c(x, c, l1_norm_mix, l1_norm_ffn, l1_mod_w, l1_mod_b, l1_w_in, l1_w_out,
                    l1_ffn_wg, l1_ffn_wu, l1_ffn_wd, final_norm)
```

```python
import functools

import numpy as np
import jax
import jax.numpy as jnp
from jax import lax
from jax.experimental import pallas as pl
from jax.experimental.pallas import tpu as pltpu

F32 = jnp.float32
BF16 = jnp.bfloat16

EPS = 1e-6
HEAD_DIM = 64
Q_BLOCK = 128
LRU_WIDTH = 512
LRU_BLOCKS = 8
CONV_WIDTH = 4
LRU_C = 8.0
NSA_HEADS = 8
NSA_KV_GROUPS = 2
NSA_HPG = NSA_HEADS // NSA_KV_GROUPS
CMP_LEN = 32
CMP_STRIDE = 16
CMP_HIDDEN = 128
SLC_LEN = 64
SLC_TOPN = 8
WIN = 512
FORCE_SCORE = 1e4
DSA_HEADS = 16
DSA_KV_HEADS = 2
DSA_HPG = DSA_HEADS // DSA_KV_HEADS
IDX_HEADS = 8
IDX_DIM = 64
IDX_TOPK_MAX = 256

LANES = 128
NEG = -1e30
VMEM_LIMIT = 56 * 1024 * 1024
INT_MIN = -2147483648
KEY_NEG_INF = int(np.array(0xFF800000 ^ 0x7FFFFFFF, np.uint32).astype(np.int32))


def _cparams(n_grid):
    return pltpu.CompilerParams(dimension_semantics=("arbitrary",) * n_grid,
                                vmem_limit_bytes=VMEM_LIMIT)


def _const_spec(shape):
    nd = len(shape)
    return pl.BlockSpec(shape, lambda *_: (0,) * nd, pipeline_mode=pl.Buffered(1))


def _dot(a, b):
    return jnp.dot(a, b, preferred_element_type=F32)


def _dot_nt(a, b):
    return lax.dot_general(a, b, (((1,), (1,)), ((), ())), preferred_element_type=F32)


def _rms(x, g):
    return x * lax.rsqrt(jnp.mean(x * x, axis=-1, keepdims=True) + EPS) * g


def _adaln_kernel(c_ref, w_ref, b_ref, o_ref):
    c = c_ref[...]
    a = (c * jax.nn.sigmoid(c)).astype(BF16)
    o_ref[...] = _dot(a, w_ref[...].astype(BF16)) + b_ref[...]


def _adaln(c, w, b):
    B, D = c.shape
    N = w.shape[1]
    tn = N // 4
    return pl.pallas_call(
        _adaln_kernel,
        grid=(N // tn,),
        in_specs=[pl.BlockSpec((B, D), lambda j: (0, 0)),
                  pl.BlockSpec((D, tn), lambda j: (0, j)),
                  pl.BlockSpec((1, tn), lambda j: (0, j))],
        out_specs=pl.BlockSpec((B, tn), lambda j: (0, j)),
        out_shape=jax.ShapeDtypeStruct((B, N), F32),
        compiler_params=_cparams(1),
        name="adaln",
    )(c, w, b.reshape(1, N))


def _inproj_kernel(x_ref, mod_ref, g_ref, w_ref, *o_refs, splits):
    x = x_ref[0]
    h = _rms(x, g_ref[...]) * (1.0 + mod_ref[0, 1:2, :]) + mod_ref[0, 0:1, :]
    acc = _dot(h.astype(BF16), w_ref[...])
    for o_ref, (c0, wd) in zip(o_refs, splits):
        o_ref[0] = acc[:, c0:c0 + wd].astype(o_ref.dtype)


def _inproj(x, mod, g, w, outs, tm=512):
    B, S, D = x.shape
    N = w.shape[1]
    splits, c0 = [], 0
    for wd, _ in outs:
        splits.append((c0, wd))
        c0 += wd
    assert c0 == N
    return pl.pallas_call(
        functools.partial(_inproj_kernel, splits=tuple(splits)),
        grid=(B, S // tm),
        in_specs=[pl.BlockSpec((1, tm, D), lambda b, i: (b, i, 0)),
                  pl.BlockSpec((1, 6, D), lambda b, i: (b, 0, 0)),
                  _const_spec((1, D)),
                  _const_spec((D, N))],
        out_specs=[pl.BlockSpec((1, tm, wd), lambda b, i: (b, i, 0)) for wd, _ in outs],
        out_shape=[jax.ShapeDtypeStruct((B, S, wd), dt) for wd, dt in outs],
        compiler_params=_cparams(2),
        name="inproj",
    )(x, mod, g.reshape(1, D), w)


def _lru_kernel(ag_ref, ax_ref, cw_ref, cb_ref, wa_ref, ba_ref, wx_ref, bx_ref, lam_ref, o_ref,
                xpad, a_s, u_s, *, S, C, TC):
    PAD = 8
    xpad[0:PAD, :] = jnp.zeros((PAD, C), F32)
    xpad[PAD:PAD + S, :] = ax_ref[0]
    z = -lam_ref[...]
    sp = jnp.maximum(z, 0.0) + jnp.log(1.0 + jnp.exp(-jnp.abs(z)))
    for ci in range(S // TC):
        r0 = ci * TC
        xc = cb_ref[...]
        for k in range(CONV_WIDTH):
            off = PAD - (CONV_WIDTH - 1) + k + r0
            xc = xc + xpad[off:off + TC, :] * cw_ref[k:k + 1, :]
        xb = xc.astype(BF16)
        r = jax.nn.sigmoid(_dot(xb, wa_ref[...]) + ba_ref[...])
        i = jax.nn.sigmoid(_dot(xb, wx_ref[...]) + bx_ref[...])
        log_a = -LRU_C * r * sp
        a = jnp.exp(log_a)
        mult = jnp.sqrt(1.0 - jnp.exp(2.0 * log_a))
        if ci == 0:
            row = lax.broadcasted_iota(jnp.int32, (TC, C), 0)
            mult = jnp.where(row == 0, 1.0, mult)
        a_s[r0:r0 + TC, :] = a
        u_s[r0:r0 + TC, :] = mult * (i * xc)

    row8 = lax.broadcasted_iota(jnp.int32, (8, C), 0)

    def tile(ti, h):
        t0 = pl.multiple_of(ti * 8, 8)
        A = a_s[pl.ds(t0, 8), :]
        U = u_s[pl.ds(t0, 8), :]
        for d in (1, 2, 4):
            As = pltpu.roll(A, d, 0)
            Us = pltpu.roll(U, d, 0)
            ok = row8 >= d
            U = jnp.where(ok, A * Us + U, U)
            A = jnp.where(ok, A * As, A)
        H = U + A * h
        u_s[pl.ds(t0, 8), :] = H
        return H[7:8, :]

    lax.fori_loop(0, S // 8, tile, jnp.zeros((1, C), F32))

    for ci in range(S // TC):
        r0 = ci * TC
        o_ref[0, r0:r0 + TC, :] = (u_s[r0:r0 + TC, :]
                                   * jax.nn.gelu(ag_ref[0, r0:r0 + TC, :])).astype(o_ref.dtype)


def _lru(ag, ax, cw, cb, wa_bd, ba, wx_bd, bx, lam):
    B, S, C = ag.shape
    TC = 256
    row = lambda v: v.reshape(1, C)
    return pl.pallas_call(
        functools.partial(_lru_kernel, S=S, C=C, TC=TC),
        grid=(B,),
        in_specs=[pl.BlockSpec((1, S, C), lambda b: (b, 0, 0)),
                  pl.BlockSpec((1, S, C), lambda b: (b, 0, 0)),
                  _const_spec((CONV_WIDTH, C)), _const_spec((1, C)),
                  _const_spec((C, C)), _const_spec((1, C)),
                  _const_spec((C, C)), _const_spec((1, C)), _const_spec((1, C))],
        out_specs=pl.BlockSpec((1, S, C), lambda b: (b, 0, 0)),
        out_shape=jax.ShapeDtypeStruct((B, S, C), BF16),
        scratch_shapes=[pltpu.VMEM((S + 8, C), F32), pltpu.VMEM((S, C), F32), pltpu.VMEM((S, C), F32)],
        compiler_params=_cparams(1),
        name="rglru",
    )(ag, ax, cw, row(cb), wa_bd, row(ba), wx_bd, row(bx), row(lam))


def _compress_kernel(xk_ref, xv_ref, pek_ref, w1k_ref, w2k_ref, pev_ref, w1v_ref, w2v_ref,
                     ok_ref, ov_ref):
    def one(x_ref, pe_ref, w1_ref, w2_ref, o_ref):
        x = x_ref[0]
        p0 = _dot((x + pe_ref[0:1, :]).astype(BF16), w1_ref[0])
        p1 = _dot((x + pe_ref[1:2, :]).astype(BF16), w1_ref[1])
        nc = x.shape[0]
        hid = p0 + pltpu.roll(p1, nc - 1, 0)
        o_ref[0] = _dot(jax.nn.gelu(hid).astype(BF16), w2_ref[...]).astype(o_ref.dtype)

    one(xk_ref, pek_ref, w1k_ref, w2k_ref, ok_ref)
    one(xv_ref, pev_ref, w1v_ref, w2v_ref, ov_ref)


def _compress(xk, xv, pek, w1k, w2k, pev, w1v, w2v):
    B, NC, W = xk.shape
    GH = NSA_KV_GROUPS * CMP_HIDDEN
    GD = NSA_KV_GROUPS * HEAD_DIM
    xspec = pl.BlockSpec((1, NC, W), lambda b: (b, 0, 0))
    ospec = pl.BlockSpec((1, NC, GD), lambda b: (b, 0, 0))
    return pl.pallas_call(
        _compress_kernel,
        grid=(B,),
        in_specs=[xspec, xspec,
                  _const_spec((2, W)), _const_spec((2, W, GH)), _const_spec((GH, GD)),
                  _const_spec((2, W)), _const_spec((2, W, GH)), _const_spec((GH, GD))],
        out_specs=[ospec, ospec],
        out_shape=[jax.ShapeDtypeStruct((B, NC, GD), BF16)] * 2,
        compiler_params=_cparams(1),
        name="nsa_compress",
    )(xk, xv, pek, w1k, w2k, pev, w1v, w2v)


def _causal_buckets(S):
    step = min(S, 512)
    assert S % step == 0
    return tuple(range(step, S + 1, step))


def _for_bucket(needed, buckets, body):
    lo = 0
    for lk in buckets:
        pl.when((needed > lo) & (needed <= lk))(functools.partial(body, lk))
        lo = lk


def _eye(QB):
    return (lax.broadcasted_iota(jnp.int32, (QB, LANES), 0)
            == lax.broadcasted_iota(jnp.int32, (QB, LANES), 1)).astype(BF16)


def _q_rows(slot, keep, eye):
    return jnp.concatenate([jnp.where(keep, slot, jnp.zeros((), BF16)), eye], axis=1)


def _colsum(x):
    n, q = x.shape
    part = jnp.sum(x.reshape(n // 64, 64, q), axis=0)
    return jnp.sum(part, axis=0, keepdims=True)


def _attend(qa, kaug, vaug):
    s = _dot_nt(qa, kaug)
    m = jnp.max(s, axis=-1, keepdims=True)
    oa = _dot(jnp.exp(s - m).astype(BF16), vaug)
    return oa[:, 0:LANES], oa[:, LANES:2 * LANES]


def _nsa_body(LK, q_ref, gt_ref, kc_ref, vc_ref, kv_ref, ovt_ref, expt_ref, o_ref, *, QB, NC, NSEL, NTOP):
    ib = pl.program_id(1)
    t0 = ib * QB
    R = NSA_HPG
    G = NSA_KV_GROUPS
    low = lax.broadcasted_iota(jnp.int32, (QB, LANES), 1) < HEAD_DIM
    keep_g = (low, jnp.logical_not(low))
    eye = _eye(QB)
    q = q_ref[0] * jnp.asarray(HEAD_DIM ** -0.5, BF16)
    gates = jax.nn.sigmoid(gt_ref[0])
    qa_g = [jnp.concatenate([_q_rows(q[:, r * LANES:(r + 1) * LANES], keep_g[g], eye) for r in range(R)], axis=0)
            for g in range(G)]
    qa = jnp.concatenate(qa_g, axis=0)

    def tq_t(n):
        return t0 + lax.broadcasted_iota(jnp.int32, (n, QB), 1)

    def vaug(v):
        return jnp.concatenate([v, jnp.ones(v.shape, BF16)], axis=1)

    cend = lax.broadcasted_iota(jnp.int32, (NC, QB), 0) * CMP_STRIDE + (CMP_LEN - 1)
    bias_c = jnp.where(cend <= tq_t(NC), 0.0, NEG).astype(BF16)
    s = _dot_nt(qa, jnp.concatenate([kc_ref[0], bias_c], axis=1))
    m = jnp.max(s, axis=-1, keepdims=True)
    p = jnp.exp(s - m)
    inv = jnp.where(m > 0.5 * NEG, 1.0 / jnp.maximum(jnp.sum(p, axis=-1, keepdims=True), 1e-30), 0.0)
    o_cmp = _dot(p.astype(BF16), vc_ref[0]) * inv
    pn = p * inv

    start = pl.multiple_of(jnp.maximum(t0 - WIN, 0), QB)
    kpos_w = start + lax.broadcasted_iota(jnp.int32, (WIN + QB, QB), 0)
    bias_w = jnp.where((kpos_w <= tq_t(WIN + QB)) & (kpos_w > tq_t(WIN + QB) - WIN), 0.0, NEG).astype(BF16)
    kwin = kv_ref[0, pl.ds(start, WIN + QB), 2 * LANES:3 * LANES]
    vwin = kv_ref[0, pl.ds(start, WIN + QB), 3 * LANES:4 * LANES]
    ow, lw = _attend(qa, jnp.concatenate([kwin, bias_w], axis=1), vaug(vwin))
    o_win = ow / lw

    jrow = lax.broadcasted_iota(jnp.int32, (LANES, QB), 0)
    cur = lax.shift_right_logical(tq_t(LANES), 6)
    forced = (jrow == 0) | (jrow == cur) | (jrow == cur - 1)
    causal_s = lax.broadcasted_iota(jnp.int32, (LK, QB), 0) <= tq_t(LK)
    kslc = kv_ref[0, 0:LK, 0:LANES]
    vslc = vaug(kv_ref[0, 0:LK, LANES:2 * LANES])
    o_slc = []
    for g in range(G):
        base = g * R * QB
        psum = pn[base:base + QB]
        for r in range(1, R):
            psum = psum + pn[base + r * QB:base + (r + 1) * QB]
        hi = psum.astype(BF16)
        lo = (psum - hi.astype(F32)).astype(BF16)
        imp = _dot_nt(ovt_ref[...], hi) + _dot_nt(ovt_ref[...], lo)
        imp = jnp.where(forced, FORCE_SCORE, imp)
        imp = jnp.where(jrow <= cur, imp, -jnp.inf)
        imp = imp[0:NSEL]
        jr = jrow[0:NSEL]
        cnt = jnp.zeros((NSEL, QB), F32)
        for j2 in range(NSEL):
            rowv = imp[j2:j2 + 1, :]
            cnt = cnt + jnp.where(jr > j2, jnp.where(rowv >= imp, 1.0, 0.0), jnp.where(rowv > imp, 1.0, 0.0))
        sel_t = jnp.where(cnt < NTOP, 1.0, 0.0)
        if NSEL < LANES:
            sel_t = jnp.concatenate([sel_t, jnp.zeros((LANES - NSEL, QB), F32)], axis=0)
        picked = _dot(expt_ref[0:LK, :], sel_t.astype(BF16))
        bias_s = jnp.where((picked > 0.5) & causal_s, 0.0, NEG).astype(BF16)
        os_, ls = _attend(qa_g[g], jnp.concatenate([kslc, bias_s], axis=1), vslc)
        o_slc.append(os_ / ls)

    for r in range(R):
        acc = jnp.zeros((QB, LANES), F32)
        ra, rb = r * QB, (R + r) * QB
        branches = ((o_cmp[ra:ra + QB], o_cmp[rb:rb + QB]),
                    (o_slc[0][ra:ra + QB], o_slc[1][ra:ra + QB]),
                    (o_win[ra:ra + QB], o_win[rb:rb + QB]))
        for j, (va, vb) in enumerate(branches):
            ca, cb = 3 * r + j, 3 * (R + r) + j
            gate = jnp.where(low, gates[:, ca:ca + 1], gates[:, cb:cb + 1])
            acc = acc + gate * jnp.where(low, va, vb)
        o_ref[0, :, r * LANES:(r + 1) * LANES] = acc.astype(o_ref.dtype)


def _nsa_kernel(*refs, QB, buckets, **kw):
    needed = (pl.program_id(1) + 1) * QB
    _for_bucket(needed, buckets, lambda lk: _nsa_body(lk, *refs, QB=QB, **kw))


def _nsa(q, gts, kc, vc, kv4):
    B, S, _ = q.shape
    QB = Q_BLOCK
    NC = kc.shape[1]
    NSEL = S // SLC_LEN
    NTOP = min(SLC_TOPN, NSEL)
    assert NSEL <= LANES and S % QB == 0 and S >= WIN + QB
    c = np.arange(NC)[None, :] * CMP_STRIDE
    j = np.arange(LANES)[:, None] * SLC_LEN
    valid_c = np.arange(NC)[None, :] < (S - CMP_LEN) // CMP_STRIDE + 1
    ovt = ((c < j + SLC_LEN) & (c + CMP_LEN > j) & valid_c & (np.arange(LANES)[:, None] < NSEL))
    expand_t = (np.arange(S)[:, None] // SLC_LEN) == np.arange(LANES)[None, :]
    GD = NSA_KV_GROUPS * HEAD_DIM
    return pl.pallas_call(
        functools.partial(_nsa_kernel, QB=QB, NC=NC, NSEL=NSEL, NTOP=NTOP, buckets=_causal_buckets(S)),
        grid=(B, S // QB),
        in_specs=[pl.BlockSpec((1, QB, NSA_HPG * LANES), lambda b, i: (b, i, 0)),
                  pl.BlockSpec((1, QB, LANES), lambda b, i: (b, i, 0)),
                  pl.BlockSpec((1, NC, GD), lambda b, i: (b, 0, 0)),
                  pl.BlockSpec((1, NC, GD), lambda b, i: (b, 0, 0)),
                  pl.BlockSpec((1, S, 4 * GD), lambda b, i: (b, 0, 0)),
                  _const_spec((LANES, NC)), _const_spec((S, LANES))],
        out_specs=pl.BlockSpec((1, QB, NSA_HPG * LANES), lambda b, i: (b, i, 0)),
        out_shape=jax.ShapeDtypeStruct((B, S, NSA_HPG * LANES), BF16),
        compiler_params=_cparams(2),
        name="nsa_attention",
    )(q, gts, kc, vc, kv4, jnp.asarray(ovt, BF16), jnp.asarray(expand_t, BF16))


def _dsa_body(LK, q_ref, k_ref, v_ref, qi_ref, ki_ref, wi_ref, o_ref, *, QB, NKEEP, CH):
    ib = pl.program_id(1)
    t0 = ib * QB
    low = lax.broadcasted_iota(jnp.int32, (QB, LANES), 1) < HEAD_DIM
    not_low = jnp.logical_not(low)
    zero = jnp.zeros((), BF16)
    causal = (lax.broadcasted_iota(jnp.int32, (LK, QB), 0)
              <= t0 + lax.broadcasted_iota(jnp.int32, (LK, QB), 1))

    w_t = (wi_ref[0] * (IDX_DIM ** -0.5 * IDX_HEADS ** -0.5)).T
    qi = qi_ref[0]
    ki = ki_ref[0, 0:LK, :]
    score = None
    for u in range(IDX_HEADS // 2):
        slot = qi[:, u * LANES:(u + 1) * LANES]
        pair = jnp.concatenate([jnp.where(low, slot, zero), jnp.where(not_low, slot, zero)], axis=0)
        sc = jnp.maximum(_dot_nt(ki, pair), 0.0)
        term = sc[:, 0:QB] * w_t[2 * u:2 * u + 1, :] + sc[:, QB:2 * QB] * w_t[2 * u + 1:2 * u + 2, :]
        score = term if score is None else score + term

    bits = pltpu.bitcast(score, jnp.int32)
    key = bits ^ (lax.shift_right_arithmetic(bits, 31) & 0x7FFFFFFF)
    key = jnp.where(causal, key, KEY_NEG_INF)

    def bit_step(i, t_u):
        c_u = t_u | lax.shift_left(jnp.int32(1), 31 - i)
        cnt = _colsum(jnp.where(key >= (c_u ^ INT_MIN), 1.0, 0.0))
        return jnp.where(cnt >= NKEEP, c_u, t_u)

    thr = lax.fori_loop(0, 32, bit_step, jnp.zeros((1, QB), jnp.int32)) ^ INT_MIN
    gt = key > thr
    eq = key == thr
    need = NKEEP - _colsum(jnp.where(gt, 1.0, 0.0))
    eq_b = jnp.where(eq, 1.0, 0.0).astype(BF16)
    tri = (lax.broadcasted_iota(jnp.int32, (CH, CH), 0)
           > lax.broadcasted_iota(jnp.int32, (CH, CH), 1)).astype(BF16)
    ranks, before = [], jnp.zeros((1, QB), F32)
    for c in range(LK // CH):
        e = eq_b[c * CH:(c + 1) * CH]
        ranks.append(_dot(tri, e) + before)
        before = before + _colsum(e.astype(F32))
    rank = jnp.concatenate(ranks, axis=0)
    keep = (gt | (eq & (rank < need))) & causal
    kaug = jnp.concatenate([k_ref[0, 0:LK, :], jnp.where(keep, 0.0, NEG).astype(BF16)], axis=1)
    vaug = jnp.concatenate([v_ref[0, 0:LK, :], jnp.ones((LK, LANES), BF16)], axis=1)

    eye = _eye(QB)
    scale = jnp.asarray(HEAD_DIM ** -0.5, BF16)
    HALF = DSA_HPG // 2
    for half in range(2):
        rows = []
        for r in range(half * HALF, (half + 1) * HALF):
            slot = q_ref[0, :, r * LANES:(r + 1) * LANES] * scale
            rows += [_q_rows(slot, low, eye), _q_rows(slot, not_low, eye)]
        o, l = _attend(jnp.concatenate(rows, axis=0), kaug, vaug)
        o = o / l
        for i in range(HALF):
            r = half * HALF + i
            o_ref[0, :, r * LANES:(r + 1) * LANES] = jnp.where(
                low, o[2 * i * QB:(2 * i + 1) * QB], o[(2 * i + 1) * QB:(2 * i + 2) * QB]).astype(o_ref.dtype)


def _dsa_kernel(*refs, QB, buckets, **kw):
    needed = (pl.program_id(1) + 1) * QB
    _for_bucket(needed, buckets, lambda lk: _dsa_body(lk, *refs, QB=QB, **kw))


def _dsa(q, k, v, qi, ki, wi):
    B, S, _ = q.shape
    QB = Q_BLOCK
    NKEEP = min(IDX_TOPK_MAX, S // 4)
    GD = DSA_KV_HEADS * HEAD_DIM
    blk = lambda w: pl.BlockSpec((1, QB, w), lambda b, i: (b, i, 0))
    full = lambda w: pl.BlockSpec((1, S, w), lambda b, i: (b, 0, 0))
    return pl.pallas_call(
        functools.partial(_dsa_kernel, QB=QB, NKEEP=NKEEP, CH=256, buckets=_causal_buckets(S)),
        grid=(B, S // QB),
        in_specs=[blk(DSA_HPG * LANES), full(GD), full(GD), blk(IDX_HEADS * IDX_DIM), full(LANES), blk(LANES)],
        out_specs=blk(DSA_HPG * LANES),
        out_shape=jax.ShapeDtypeStruct((B, S, DSA_HPG * LANES), BF16),
        compiler_params=_cparams(2),
        name="dsa_attention",
    )(q, k, v, qi, ki, wi)


def _mix_ffn_kernel(*refs, n_y, final):
    x_ref, mod_ref, gf_ref = refs[0:3]
    y_refs = refs[3:3 + n_y]
    wo_refs = refs[3 + n_y:3 + 2 * n_y]
    wg_ref, wu_ref, wd_ref = refs[3 + 2 * n_y:6 + 2 * n_y]
    rest = refs[6 + 2 * n_y:]
    o_ref = rest[-1]
    mix = None
    for y_ref, wo_ref in zip(y_refs, wo_refs):
        t = _dot(y_ref[0], wo_ref[...])
        mix = t if mix is None else mix + t
    x1 = x_ref[0] + mod_ref[0, 2:3, :] * mix
    h = (_rms(x1, gf_ref[...]) * (1.0 + mod_ref[0, 4:5, :]) + mod_ref[0, 3:4, :]).astype(BF16)
    gate = _dot(h, wg_ref[...])
    up = _dot(h, wu_ref[...])
    act = (gate * jax.nn.sigmoid(gate) * up).astype(BF16)
    x2 = x1 + mod_ref[0, 5:6, :] * _dot(act, wd_ref[...])
    if final:
        x2 = _rms(x2, rest[0][...])
    o_ref[0] = x2


def _mix_ffn(x, mod, g_ffn, ys, wos, wg, wu, wd, g_final=None, tm=512):
    B, S, D = x.shape
    FF = wg.shape[1]
    final = g_final is not None
    row_blk = lambda w: pl.BlockSpec((1, tm, w), lambda b, i: (b, i, 0))
    in_specs = ([row_blk(D), pl.BlockSpec((1, 6, D), lambda b, i: (b, 0, 0)), _const_spec((1, D))]
                + [row_blk(y.shape[2]) for y in ys]
                + [_const_spec(w.shape) for w in wos]
                + [_const_spec((D, FF)), _const_spec((D, FF)), _const_spec((FF, D))])
    args = [x, mod, g_ffn.reshape(1, D), *ys, *wos, wg, wu, wd]
    if final:
        in_specs.append(_const_spec((1, D)))
        args.append(g_final.reshape(1, D))
    return pl.pallas_call(
        functools.partial(_mix_ffn_kernel, n_y=len(ys), final=final),
        grid=(B, S // tm),
        in_specs=in_specs,
        out_specs=row_blk(D),
        out_shape=jax.ShapeDtypeStruct((B, S, D), F32),
        compiler_params=_cparams(2),
        name="mix_ffn",
    )(*args)


def _block_diag(w):
    nb, bi, bo = w.shape
    out = jnp.zeros((nb * bi, nb * bo), w.dtype)
    for h in range(nb):
        out = out.at[h * bi:(h + 1) * bi, h * bo:(h + 1) * bo].set(w[h])
    return out


def _pair_heads(n_groups, per_group):
    assert n_groups == 2
    return [g * per_group + r for r in range(per_group) for g in range(n_groups)]


def _head_cols(order):
    return np.concatenate([np.arange(h * HEAD_DIM, (h + 1) * HEAD_DIM) for h in order])


def _pad_cols(w, n):
    return jnp.pad(w, ((0, 0), (0, n - w.shape[1])))


def _compress_weights(pe, w1, w2):
    G, D, H = NSA_KV_GROUPS, HEAD_DIM, CMP_HIDDEN
    half = CMP_LEN // 2
    pe_r = jnp.broadcast_to(pe.reshape(2, half, 1, D), (2, half, G, D)).reshape(2, half * G * D)
    w1r = w1.reshape(2, half, D, H)
    w1e = jnp.zeros((2, half, G, D, G, H), w1.dtype)
    for g in range(G):
        w1e = w1e.at[:, :, g, :, g, :].set(w1r)
    w1e = w1e.reshape(2, half * G * D, G * H)
    w2e = jnp.zeros((G, H, G, D), w2.dtype)
    for g in range(G):
        w2e = w2e.at[g, :, g, :].set(w2)
    return pe_r, w1e.astype(BF16), w2e.reshape(G * H, G * D).astype(BF16)


def _layer_ab(x, c, norm_mix, norm_ffn, mod_w, mod_b, w_in, conv_w, conv_b, wa, ba, wx, bx, lam,
              pe_k, w1_k, w2_k, pe_v, w1_v, w2_v, w_out, wg, wu, wd):
    B, S, D = x.shape
    mod = _adaln(c, mod_w, mod_b).reshape(B, 6, D)
    C = LRU_WIDTH
    HQ = NSA_HEADS * HEAD_DIM
    GD = NSA_KV_GROUPS * HEAD_DIM
    order = _pair_heads(NSA_KV_GROUPS, NSA_HPG)
    o_q = 2 * C
    o_kv = o_q + HQ
    o_gt = o_kv + 6 * GD
    w_q = w_in[:, o_q:o_kv][:, _head_cols(order)]
    w_all = jnp.concatenate([w_in[:, :o_q], w_q, w_in[:, o_kv:o_gt], _pad_cols(w_in[:, o_gt:], LANES)],
                            axis=1).astype(BF16)
    ag, ax, q, kcmp, vcmp, kv4, gts = _inproj(
        x, mod, norm_mix, w_all,
        [(C, F32), (C, F32), (HQ, BF16), (GD, F32), (GD, F32), (4 * GD, BF16), (LANES, F32)])
    y_a = _lru(ag, ax, conv_w, conv_b, _block_diag(wa).astype(BF16), ba, _block_diag(wx).astype(BF16), bx, lam)
    NC = S // CMP_STRIDE
    kc, vc = _compress(kcmp.reshape(B, NC, CMP_STRIDE * GD), vcmp.reshape(B, NC, CMP_STRIDE * GD),
                       *_compress_weights(pe_k, w1_k, w2_k), *_compress_weights(pe_v, w1_v, w2_v))
    y_b = _nsa(q, gts, kc, vc, kv4)
    wo_a = w_out[:C].astype(BF16)
    wo_b = w_out[C:][_head_cols(order)].astype(BF16)
    return _mix_ffn(x, mod, norm_ffn, [y_a, y_b], [wo_a, wo_b],
                    wg.astype(BF16), wu.astype(BF16), wd.astype(BF16))


def _layer_c(x, c, norm_mix, norm_ffn, mod_w, mod_b, w_in, w_out, wg, wu, wd, final_norm):
    B, S, D = x.shape
    mod = _adaln(c, mod_w, mod_b).reshape(B, 6, D)
    HQ = DSA_HEADS * HEAD_DIM
    GD = DSA_KV_HEADS * HEAD_DIM
    HI = IDX_HEADS * IDX_DIM
    order = _pair_heads(DSA_KV_HEADS, DSA_HPG)
    o_k = HQ
    o_v = o_k + GD
    o_qi = o_v + GD
    o_ki = o_qi + HI
    o_wi = o_ki + IDX_DIM
    w_ki = w_in[:, o_ki:o_wi]
    w_all = jnp.concatenate([w_in[:, :HQ][:, _head_cols(order)], w_in[:, o_k:o_qi], w_in[:, o_qi:o_ki],
                             w_ki, w_ki, _pad_cols(w_in[:, o_wi:], LANES)], axis=1).astype(BF16)
    q, k, v, qi, ki, wi = _inproj(
        x, mod, norm_mix, w_all,
        [(HQ, BF16), (GD, BF16), (GD, BF16), (HI, BF16), (LANES, BF16), (LANES, F32)])
    y = _dsa(q, k, v, qi, ki, wi)
    wo = w_out[_head_cols(order)].astype(BF16)
    return _mix_ffn(x, mod, norm_ffn, [y], [wo], wg.astype(BF16), wu.astype(BF16), wd.astype(BF16),
                    g_final=final_norm)


def kernel(x, c, l0_norm_mix, l0_norm_ffn, l0_mod_w, l0_mod_b, l0_w_in, l0_conv_w, l0_conv_b, l0_lru_wa, l0_lru_ba, l0_lru_wx, l0_lru_bx, l0_lru_lambda, l0_cmp_pe_k, l0_cmp_w1_k, l0_cmp_w2_k, l0_cmp_pe_v, l0_cmp_w1_v, l0_cmp_w2_v, l0_w_out, l0_ffn_wg, l0_ffn_wu, l0_ffn_wd, l1_norm_mix, l1_norm_ffn, l1_mod_w, l1_mod_b, l1_w_in, l1_w_out, l1_ffn_wg, l1_ffn_wu, l1_ffn_wd, final_norm):
    x = _layer_ab(x, c, l0_norm_mix, l0_norm_ffn, l0_mod_w, l0_mod_b, l0_w_in, l0_conv_w, l0_conv_b,
                  l0_lru_wa, l0_lru_ba, l0_lru_wx, l0_lru_bx, l0_lru_lambda,
                  l0_cmp_pe_k, l0_cmp_w1_k, l0_cmp_w2_k, l0_cmp_pe_v, l0_cmp_w1_v, l0_cmp_w2_v,
                  l0_w_out, l0_ffn_wg, l0_ffn_wu, l0_ffn_wd)
    return _layer_c(x, c, l1_norm_mix, l1_norm_ffn, l1_mod_w, l1_mod_b, l1_w_in, l1_w_out,
                    l1_ffn_wg, l1_ffn_wu, l1_ffn_wd, final_norm)
```

```python
import functools

import numpy as np
import jax
import jax.numpy as jnp
from jax import lax
from jax.experimental import pallas as pl
from jax.experimental.pallas import tpu as pltpu

F32 = jnp.float32
BF16 = jnp.bfloat16

EPS = 1e-6
HEAD_DIM = 64
Q_BLOCK = 128
ATT_SUB = 2
LRU_WIDTH = 512
LRU_BLOCKS = 8
CONV_WIDTH = 4
LRU_C = 8.0
NSA_HEADS = 8
NSA_KV_GROUPS = 2
NSA_HPG = NSA_HEADS // NSA_KV_GROUPS
CMP_LEN = 32
CMP_STRIDE = 16
CMP_HIDDEN = 128
SLC_LEN = 64
SLC_TOPN = 8
WIN = 512
FORCE_SCORE = 1e4
DSA_HEADS = 16
DSA_KV_HEADS = 2
DSA_HPG = DSA_HEADS // DSA_KV_HEADS
IDX_HEADS = 8
IDX_DIM = 64
IDX_TOPK_MAX = 256

LANES = 128
NEG = -1e30
VMEM_LIMIT = 56 * 1024 * 1024
QK_SCALE = HEAD_DIM ** -0.5 * float(np.log2(np.e))
INT_MIN = -2147483648
KEY_NEG_INF = int(np.array(0xFF800000 ^ 0x7FFFFFFF, np.uint32).astype(np.int32))


def _cparams(n_grid):
    return pltpu.CompilerParams(dimension_semantics=("arbitrary",) * n_grid,
                                vmem_limit_bytes=VMEM_LIMIT)


def _const_spec(shape):
    nd = len(shape)
    return pl.BlockSpec(shape, lambda *_: (0,) * nd, pipeline_mode=pl.Buffered(1))


def _dot(a, b):
    return jnp.dot(a, b, preferred_element_type=F32)


def _dot_nt(a, b):
    return lax.dot_general(a, b, (((1,), (1,)), ((), ())), preferred_element_type=F32)


def _rms(x, g):
    return x * lax.rsqrt(jnp.mean(x * x, axis=-1, keepdims=True) + EPS) * g


def _adaln_kernel(c_ref, w_ref, b_ref, o_ref):
    c = c_ref[...]
    a = (c * jax.nn.sigmoid(c)).astype(BF16)
    o_ref[...] = _dot(a, w_ref[...].astype(BF16)) + b_ref[...]


def _adaln(c, w, b):
    B, D = c.shape
    N = w.shape[1]
    tn = N // 4
    return pl.pallas_call(
        _adaln_kernel,
        grid=(N // tn,),
        in_specs=[pl.BlockSpec((B, D), lambda j: (0, 0)),
                  pl.BlockSpec((D, tn), lambda j: (0, j)),
                  pl.BlockSpec((1, tn), lambda j: (0, j))],
        out_specs=pl.BlockSpec((B, tn), lambda j: (0, j)),
        out_shape=jax.ShapeDtypeStruct((B, N), F32),
        compiler_params=_cparams(1),
        name="adaln",
    )(c, w, b.reshape(1, N))


def _inproj_kernel(x_ref, mod_ref, g_ref, w_ref, *o_refs, splits):
    x = x_ref[0]
    h = _rms(x, g_ref[...]) * (1.0 + mod_ref[0, 1:2, :]) + mod_ref[0, 0:1, :]
    acc = _dot(h.astype(BF16), w_ref[...])
    for o_ref, (c0, wd) in zip(o_refs, splits):
        o_ref[0] = acc[:, c0:c0 + wd].astype(o_ref.dtype)


def _inproj(x, mod, g, w, outs, tm=512):
    B, S, D = x.shape
    N = w.shape[1]
    splits, c0 = [], 0
    for wd, _ in outs:
        splits.append((c0, wd))
        c0 += wd
    assert c0 == N
    return pl.pallas_call(
        functools.partial(_inproj_kernel, splits=tuple(splits)),
        grid=(B, S // tm),
        in_specs=[pl.BlockSpec((1, tm, D), lambda b, i: (b, i, 0)),
                  pl.BlockSpec((1, 6, D), lambda b, i: (b, 0, 0)),
                  _const_spec((1, D)),
                  _const_spec((D, N))],
        out_specs=[pl.BlockSpec((1, tm, wd), lambda b, i: (b, i, 0)) for wd, _ in outs],
        out_shape=[jax.ShapeDtypeStruct((B, S, wd), dt) for wd, dt in outs],
        compiler_params=_cparams(2),
        name="inproj",
    )(x, mod, g.reshape(1, D), w)


def _lru_kernel(ag_ref, ax_ref, cw_ref, cb_ref, wa_ref, ba_ref, wx_ref, bx_ref, lam_ref, o_ref,
                xpad, a_s, u_s, *, S, C, TC):
    PAD = 8
    xpad[0:PAD, :] = jnp.zeros((PAD, C), F32)
    xpad[PAD:PAD + S, :] = ax_ref[0]
    z = -lam_ref[...]
    sp = jnp.maximum(z, 0.0) + jnp.log(1.0 + jnp.exp(-jnp.abs(z)))
    for ci in range(S // TC):
        r0 = ci * TC
        xc = cb_ref[...]
        for k in range(CONV_WIDTH):
            off = PAD - (CONV_WIDTH - 1) + k + r0
            xc = xc + xpad[off:off + TC, :] * cw_ref[k:k + 1, :]
        xb = xc.astype(BF16)
        r = jax.nn.sigmoid(_dot(xb, wa_ref[...]) + ba_ref[...])
        i = jax.nn.sigmoid(_dot(xb, wx_ref[...]) + bx_ref[...])
        log_a = -LRU_C * r * sp
        a = jnp.exp(log_a)
        mult = jnp.sqrt(1.0 - jnp.exp(2.0 * log_a))
        if ci == 0:
            row = lax.broadcasted_iota(jnp.int32, (TC, C), 0)
            mult = jnp.where(row == 0, 1.0, mult)
        a_s[r0:r0 + TC, :] = a
        u_s[r0:r0 + TC, :] = mult * (i * xc)

    row8 = lax.broadcasted_iota(jnp.int32, (8, C), 0)

    def tile(ti, h):
        t0 = pl.multiple_of(ti * 8, 8)
        A = a_s[pl.ds(t0, 8), :]
        U = u_s[pl.ds(t0, 8), :]
        for d in (1, 2, 4):
            As = pltpu.roll(A, d, 0)
            Us = pltpu.roll(U, d, 0)
            ok = row8 >= d
            U = jnp.where(ok, A * Us + U, U)
            A = jnp.where(ok, A * As, A)
        H = U + A * h
        u_s[pl.ds(t0, 8), :] = H
        return H[7:8, :]

    lax.fori_loop(0, S // 8, tile, jnp.zeros((1, C), F32))

    for ci in range(S // TC):
        r0 = ci * TC
        o_ref[0, r0:r0 + TC, :] = (u_s[r0:r0 + TC, :]
                                   * jax.nn.gelu(ag_ref[0, r0:r0 + TC, :])).astype(o_ref.dtype)


def _lru(ag, ax, cw, cb, wa_bd, ba, wx_bd, bx, lam):
    B, S, C = ag.shape
    TC = 256
    row = lambda v: v.reshape(1, C)
    return pl.pallas_call(
        functools.partial(_lru_kernel, S=S, C=C, TC=TC),
        grid=(B,),
        in_specs=[pl.BlockSpec((1, S, C), lambda b: (b, 0, 0)),
                  pl.BlockSpec((1, S, C), lambda b: (b, 0, 0)),
                  _const_spec((CONV_WIDTH, C)), _const_spec((1, C)),
                  _const_spec((C, C)), _const_spec((1, C)),
                  _const_spec((C, C)), _const_spec((1, C)), _const_spec((1, C))],
        out_specs=pl.BlockSpec((1, S, C), lambda b: (b, 0, 0)),
        out_shape=jax.ShapeDtypeStruct((B, S, C), BF16),
        scratch_shapes=[pltpu.VMEM((S + 8, C), F32), pltpu.VMEM((S, C), F32), pltpu.VMEM((S, C), F32)],
        compiler_params=_cparams(1),
        name="rglru",
    )(ag, ax, cw, row(cb), wa_bd, row(ba), wx_bd, row(bx), row(lam))


def _compress_kernel(xk_ref, xv_ref, pek_ref, w1k_ref, w2k_ref, pev_ref, w1v_ref, w2v_ref,
                     ok_ref, ov_ref):
    def one(x_ref, pe_ref, w1_ref, w2_ref, o_ref):
        x = x_ref[0]
        p0 = _dot((x + pe_ref[0:1, :]).astype(BF16), w1_ref[0])
        p1 = _dot((x + pe_ref[1:2, :]).astype(BF16), w1_ref[1])
        nc = x.shape[0]
        hid = p0 + pltpu.roll(p1, nc - 1, 0)
        o_ref[0] = _dot(jax.nn.gelu(hid).astype(BF16), w2_ref[...]).astype(o_ref.dtype)

    one(xk_ref, pek_ref, w1k_ref, w2k_ref, ok_ref)
    one(xv_ref, pev_ref, w1v_ref, w2v_ref, ov_ref)


def _compress(xk, xv, pek, w1k, w2k, pev, w1v, w2v):
    B, NC, W = xk.shape
    GH = NSA_KV_GROUPS * CMP_HIDDEN
    GD = NSA_KV_GROUPS * HEAD_DIM
    xspec = pl.BlockSpec((1, NC, W), lambda b: (b, 0, 0))
    ospec = pl.BlockSpec((1, NC, GD), lambda b: (b, 0, 0))
    return pl.pallas_call(
        _compress_kernel,
        grid=(B,),
        in_specs=[xspec, xspec,
                  _const_spec((2, W)), _const_spec((2, W, GH)), _const_spec((GH, GD)),
                  _const_spec((2, W)), _const_spec((2, W, GH)), _const_spec((GH, GD))],
        out_specs=[ospec, ospec],
        out_shape=[jax.ShapeDtypeStruct((B, NC, GD), BF16)] * 2,
        compiler_params=_cparams(1),
        name="nsa_compress",
    )(xk, xv, pek, w1k, w2k, pev, w1v, w2v)


def _causal_buckets(S, step):
    step = min(S, step)
    assert S % step == 0
    return tuple(range(step, S + 1, step))


def _for_bucket(needed, buckets, body):
    lo = 0
    for lk in buckets:
        pl.when((needed > lo) & (needed <= lk))(functools.partial(body, lk))
        lo = lk


def _eye(QB):
    return (lax.broadcasted_iota(jnp.int32, (QB, LANES), 0)
            == lax.broadcasted_iota(jnp.int32, (QB, LANES), 1)).astype(BF16)


def _q_rows(slot, keep, eye):
    return jnp.concatenate([jnp.where(keep, slot, jnp.zeros((), BF16)), eye], axis=1)


def _colsum(x):
    n, q = x.shape
    part = jnp.sum(x.reshape(n // 64, 64, q), axis=0)
    return jnp.sum(part, axis=0, keepdims=True)


def _attend(qa, kaug, vaug):
    s = _dot_nt(qa, kaug)
    m = jnp.max(s, axis=-1, keepdims=True)
    oa = _dot(jnp.exp2(s - m).astype(BF16), vaug)
    return oa[:, 0:LANES], oa[:, LANES:2 * LANES]


def _nsa_body(LK, sub, q_ref, gt_ref, kc_ref, vc_ref, kv_ref, ovt_ref, expt_ref, o_ref, *, QB, NC, NSEL, NTOP):
    t0 = (pl.program_id(1) * ATT_SUB + sub) * QB
    rows = slice(sub * QB, (sub + 1) * QB)
    R = NSA_HPG
    G = NSA_KV_GROUPS
    low = lax.broadcasted_iota(jnp.int32, (QB, LANES), 1) < HEAD_DIM
    keep_g = (low, jnp.logical_not(low))
    eye = _eye(QB)
    q = q_ref[0, rows, :] * jnp.asarray(QK_SCALE, BF16)
    gates = jax.nn.sigmoid(gt_ref[0, rows, :])
    qa_g = [jnp.concatenate([_q_rows(q[:, r * LANES:(r + 1) * LANES], keep_g[g], eye) for r in range(R)], axis=0)
            for g in range(G)]
    qa = jnp.concatenate(qa_g, axis=0)

    def tq_t(n):
        return t0 + lax.broadcasted_iota(jnp.int32, (n, QB), 1)

    def vaug(v):
        return jnp.concatenate([v, jnp.ones(v.shape, BF16)], axis=1)

    cend = lax.broadcasted_iota(jnp.int32, (NC, QB), 0) * CMP_STRIDE + (CMP_LEN - 1)
    bias_c = jnp.where(cend <= tq_t(NC), 0.0, NEG).astype(BF16)
    s = _dot_nt(qa, jnp.concatenate([kc_ref[0], bias_c], axis=1))
    m = jnp.max(s, axis=-1, keepdims=True)
    p = jnp.exp2(s - m)
    inv = jnp.where(m > 0.5 * NEG, 1.0 / jnp.maximum(jnp.sum(p, axis=-1, keepdims=True), 1e-30), 0.0)
    o_cmp = _dot(p.astype(BF16), vc_ref[0]) * inv
    pn = p * inv

    start = pl.multiple_of(jnp.maximum(t0 - WIN, 0), QB)
    kpos_w = start + lax.broadcasted_iota(jnp.int32, (WIN + QB, QB), 0)
    bias_w = jnp.where((kpos_w <= tq_t(WIN + QB)) & (kpos_w > tq_t(WIN + QB) - WIN), 0.0, NEG).astype(BF16)
    kwin = kv_ref[0, pl.ds(start, WIN + QB), 2 * LANES:3 * LANES]
    vwin = kv_ref[0, pl.ds(start, WIN + QB), 3 * LANES:4 * LANES]
    ow, lw = _attend(qa, jnp.concatenate([kwin, bias_w], axis=1), vaug(vwin))
    o_win = ow / lw

    jrow = lax.broadcasted_iota(jnp.int32, (LANES, QB), 0)
    cur = lax.shift_right_logical(tq_t(LANES), 6)
    forced = (jrow == 0) | (jrow == cur) | (jrow == cur - 1)
    causal_s = lax.broadcasted_iota(jnp.int32, (LK, QB), 0) <= tq_t(LK)
    kslc = kv_ref[0, 0:LK, 0:LANES]
    vslc = vaug(kv_ref[0, 0:LK, LANES:2 * LANES])
    o_slc = []
    for g in range(G):
        base = g * R * QB
        psum = pn[base:base + QB]
        for r in range(1, R):
            psum = psum + pn[base + r * QB:base + (r + 1) * QB]
        hi = psum.astype(BF16)
        lo = (psum - hi.astype(F32)).astype(BF16)
        imp = _dot_nt(ovt_ref[...], hi) + _dot_nt(ovt_ref[...], lo)
        imp = jnp.where(forced, FORCE_SCORE, imp)
        imp = jnp.where(jrow <= cur, imp, -jnp.inf)
        imp = imp[0:NSEL]
        jr = jrow[0:NSEL]
        cnt = jnp.zeros((NSEL, QB), F32)
        for j2 in range(NSEL):
            rowv = imp[j2:j2 + 1, :]
            cnt = cnt + jnp.where(jr > j2, jnp.where(rowv >= imp, 1.0, 0.0), jnp.where(rowv > imp, 1.0, 0.0))
        sel_t = jnp.where(cnt < NTOP, 1.0, 0.0)
        if NSEL < LANES:
            sel_t = jnp.concatenate([sel_t, jnp.zeros((LANES - NSEL, QB), F32)], axis=0)
        picked = _dot(expt_ref[0:LK, :], sel_t.astype(BF16))
        bias_s = jnp.where((picked > 0.5) & causal_s, 0.0, NEG).astype(BF16)
        os_, ls = _attend(qa_g[g], jnp.concatenate([kslc, bias_s], axis=1), vslc)
        o_slc.append(os_ / ls)

    for r in range(R):
        acc = jnp.zeros((QB, LANES), F32)
        ra, rb = r * QB, (R + r) * QB
        branches = ((o_cmp[ra:ra + QB], o_cmp[rb:rb + QB]),
                    (o_slc[0][ra:ra + QB], o_slc[1][ra:ra + QB]),
                    (o_win[ra:ra + QB], o_win[rb:rb + QB]))
        for j, (va, vb) in enumerate(branches):
            ca, cb = 3 * r + j, 3 * (R + r) + j
            gate = jnp.where(low, gates[:, ca:ca + 1], gates[:, cb:cb + 1])
            acc = acc + gate * jnp.where(low, va, vb)
        o_ref[0, rows, r * LANES:(r + 1) * LANES] = acc.astype(o_ref.dtype)


def _nsa_kernel(*refs, QB, buckets, **kw):
    needed = (pl.program_id(1) + 1) * ATT_SUB * QB

    def body(lk):
        for sub in range(ATT_SUB):
            _nsa_body(lk, sub, *refs, QB=QB, **kw)

    _for_bucket(needed, buckets, body)


def _nsa(q, gts, kc, vc, kv4):
    B, S, _ = q.shape
    QB = Q_BLOCK
    NC = kc.shape[1]
    NSEL = S // SLC_LEN
    NTOP = min(SLC_TOPN, NSEL)
    TQ = ATT_SUB * QB
    assert NSEL <= LANES and S % TQ == 0 and S >= WIN + QB
    c = np.arange(NC)[None, :] * CMP_STRIDE
    j = np.arange(LANES)[:, None] * SLC_LEN
    valid_c = np.arange(NC)[None, :] < (S - CMP_LEN) // CMP_STRIDE + 1
    ovt = ((c < j + SLC_LEN) & (c + CMP_LEN > j) & valid_c & (np.arange(LANES)[:, None] < NSEL))
    expand_t = (np.arange(S)[:, None] // SLC_LEN) == np.arange(LANES)[None, :]
    GD = NSA_KV_GROUPS * HEAD_DIM
    return pl.pallas_call(
        functools.partial(_nsa_kernel, QB=QB, NC=NC, NSEL=NSEL, NTOP=NTOP, buckets=_causal_buckets(S, 512)),
        grid=(B, S // TQ),
        in_specs=[pl.BlockSpec((1, TQ, NSA_HPG * LANES), lambda b, i: (b, i, 0)),
                  pl.BlockSpec((1, TQ, LANES), lambda b, i: (b, i, 0)),
                  pl.BlockSpec((1, NC, GD), lambda b, i: (b, 0, 0)),
                  pl.BlockSpec((1, NC, GD), lambda b, i: (b, 0, 0)),
                  pl.BlockSpec((1, S, 4 * GD), lambda b, i: (b, 0, 0)),
                  _const_spec((LANES, NC)), _const_spec((S, LANES))],
        out_specs=pl.BlockSpec((1, TQ, NSA_HPG * LANES), lambda b, i: (b, i, 0)),
        out_shape=jax.ShapeDtypeStruct((B, S, NSA_HPG * LANES), BF16),
        compiler_params=_cparams(2),
        name="nsa_attention",
    )(q, gts, kc, vc, kv4, jnp.asarray(ovt, BF16), jnp.asarray(expand_t, BF16))


def _dsa_body(LK, q_ref, k_ref, v_ref, qi_ref, ki_ref, wi_ref, o_ref, *, QB, NKEEP, CH):
    TQ = ATT_SUB * QB
    t0 = pl.program_id(1) * TQ
    low_t = lax.broadcasted_iota(jnp.int32, (TQ, LANES), 1) < HEAD_DIM
    zero = jnp.zeros((), BF16)
    causal = (lax.broadcasted_iota(jnp.int32, (LK, TQ), 0)
              <= t0 + lax.broadcasted_iota(jnp.int32, (LK, TQ), 1))

    w_t = (wi_ref[0] * (IDX_DIM ** -0.5 * IDX_HEADS ** -0.5)).T
    qi = qi_ref[0]
    ki = ki_ref[0, 0:LK, :]
    score = None
    for u in range(IDX_HEADS // 2):
        slot = qi[:, u * LANES:(u + 1) * LANES]
        pair = jnp.concatenate([jnp.where(low_t, slot, zero), jnp.where(low_t, zero, slot)], axis=0)
        sc = jnp.maximum(_dot_nt(ki, pair), 0.0)
        term = sc[:, 0:TQ] * w_t[2 * u:2 * u + 1, :] + sc[:, TQ:2 * TQ] * w_t[2 * u + 1:2 * u + 2, :]
        score = term if score is None else score + term

    bits = pltpu.bitcast(score, jnp.int32)
    key = bits ^ (lax.shift_right_arithmetic(bits, 31) & 0x7FFFFFFF)
    key = jnp.where(causal, key, KEY_NEG_INF)

    def bit_step(i, t_u):
        c_u = t_u | lax.shift_left(jnp.int32(1), 31 - i)
        cnt = _colsum(jnp.where(key >= (c_u ^ INT_MIN), 1.0, 0.0))
        return jnp.where(cnt >= NKEEP, c_u, t_u)

    thr = lax.fori_loop(0, 32, bit_step, jnp.zeros((1, TQ), jnp.int32)) ^ INT_MIN
    gt = key > thr
    eq = key == thr
    need = NKEEP - _colsum(jnp.where(gt, 1.0, 0.0))
    eq_b = jnp.where(eq, 1.0, 0.0).astype(BF16)
    tri = (lax.broadcasted_iota(jnp.int32, (CH, CH), 0)
           > lax.broadcasted_iota(jnp.int32, (CH, CH), 1)).astype(BF16)
    ranks, before = [], jnp.zeros((1, TQ), F32)
    for c in range(LK // CH):
        e = eq_b[c * CH:(c + 1) * CH]
        ranks.append(_dot(tri, e) + before)
        before = before + _colsum(e.astype(F32))
    rank = jnp.concatenate(ranks, axis=0)
    bias = jnp.where((gt | (eq & (rank < need))) & causal, 0.0, NEG).astype(BF16)
    k = k_ref[0, 0:LK, :]
    vaug = jnp.concatenate([v_ref[0, 0:LK, :], jnp.ones((LK, LANES), BF16)], axis=1)

    low = lax.broadcasted_iota(jnp.int32, (QB, LANES), 1) < HEAD_DIM
    not_low = jnp.logical_not(low)
    eye = _eye(QB)
    scale = jnp.asarray(QK_SCALE, BF16)
    HALF = DSA_HPG // 2
    for sub in range(ATT_SUB):
        rows_q = slice(sub * QB, (sub + 1) * QB)
        kaug = jnp.concatenate([k, bias[:, sub * QB:(sub + 1) * QB]], axis=1)
        for half in range(2):
            rows = []
            for r in range(half * HALF, (half + 1) * HALF):
                slot = q_ref[0, rows_q, r * LANES:(r + 1) * LANES] * scale
                rows += [_q_rows(slot, low, eye), _q_rows(slot, not_low, eye)]
            o, l = _attend(jnp.concatenate(rows, axis=0), kaug, vaug)
            o = o / l
            for i in range(HALF):
                r = half * HALF + i
                o_ref[0, rows_q, r * LANES:(r + 1) * LANES] = jnp.where(
                    low, o[2 * i * QB:(2 * i + 1) * QB], o[(2 * i + 1) * QB:(2 * i + 2) * QB]).astype(o_ref.dtype)


def _dsa_kernel(*refs, QB, buckets, **kw):
    needed = (pl.program_id(1) + 1) * ATT_SUB * QB
    _for_bucket(needed, buckets, lambda lk: _dsa_body(lk, *refs, QB=QB, **kw))


def _dsa(q, k, v, qi, ki, wi):
    B, S, _ = q.shape
    QB = Q_BLOCK
    NKEEP = min(IDX_TOPK_MAX, S // 4)
    GD = DSA_KV_HEADS * HEAD_DIM
    TQ = ATT_SUB * QB
    assert S % TQ == 0
    blk = lambda w: pl.BlockSpec((1, TQ, w), lambda b, i: (b, i, 0))
    full = lambda w: pl.BlockSpec((1, S, w), lambda b, i: (b, 0, 0))
    return pl.pallas_call(
        functools.partial(_dsa_kernel, QB=QB, NKEEP=NKEEP, CH=256, buckets=_causal_buckets(S, TQ)),
        grid=(B, S // TQ),
        in_specs=[blk(DSA_HPG * LANES), full(GD), full(GD), blk(IDX_HEADS * IDX_DIM), full(LANES), blk(LANES)],
        out_specs=blk(DSA_HPG * LANES),
        out_shape=jax.ShapeDtypeStruct((B, S, DSA_HPG * LANES), BF16),
        compiler_params=_cparams(2),
        name="dsa_attention",
    )(q, k, v, qi, ki, wi)


def _mix_ffn_kernel(*refs, n_y, final):
    x_ref, mod_ref, gf_ref = refs[0:3]
    y_refs = refs[3:3 + n_y]
    wo_refs = refs[3 + n_y:3 + 2 * n_y]
    wg_ref, wu_ref, wd_ref = refs[3 + 2 * n_y:6 + 2 * n_y]
    rest = refs[6 + 2 * n_y:]
    o_ref = rest[-1]
    mix = None
    for y_ref, wo_ref in zip(y_refs, wo_refs):
        t = _dot(y_ref[0], wo_ref[...])
        mix = t if mix is None else mix + t
    x1 = x_ref[0] + mod_ref[0, 2:3, :] * mix
    h = (_rms(x1, gf_ref[...]) * (1.0 + mod_ref[0, 4:5, :]) + mod_ref[0, 3:4, :]).astype(BF16)
    gate = _dot(h, wg_ref[...])
    up = _dot(h, wu_ref[...])
    act = (gate * jax.nn.sigmoid(gate) * up).astype(BF16)
    x2 = x1 + mod_ref[0, 5:6, :] * _dot(act, wd_ref[...])
    if final:
        x2 = _rms(x2, rest[0][...])
    o_ref[0] = x2


def _mix_ffn(x, mod, g_ffn, ys, wos, wg, wu, wd, g_final=None, tm=512):
    B, S, D = x.shape
    FF = wg.shape[1]
    final = g_final is not None
    row_blk = lambda w: pl.BlockSpec((1, tm, w), lambda b, i: (b, i, 0))
    in_specs = ([row_blk(D), pl.BlockSpec((1, 6, D), lambda b, i: (b, 0, 0)), _const_spec((1, D))]
                + [row_blk(y.shape[2]) for y in ys]
                + [_const_spec(w.shape) for w in wos]
                + [_const_spec((D, FF)), _const_spec((D, FF)), _const_spec((FF, D))])
    args = [x, mod, g_ffn.reshape(1, D), *ys, *wos, wg, wu, wd]
    if final:
        in_specs.append(_const_spec((1, D)))
        args.append(g_final.reshape(1, D))
    return pl.pallas_call(
        functools.partial(_mix_ffn_kernel, n_y=len(ys), final=final),
        grid=(B, S // tm),
        in_specs=in_specs,
        out_specs=row_blk(D),
        out_shape=jax.ShapeDtypeStruct((B, S, D), F32),
        compiler_params=_cparams(2),
        name="mix_ffn",
    )(*args)


def _block_diag(w):
    nb, bi, bo = w.shape
    eye = jnp.eye(nb, dtype=w.dtype)
    return (w[:, :, None, :] * eye[:, None, :, None]).reshape(nb * bi, nb * bo)


def _pair_heads(n_groups, per_group):
    assert n_groups == 2
    return [g * per_group + r for r in range(per_group) for g in range(n_groups)]


def _head_cols(order):
    return np.concatenate([np.arange(h * HEAD_DIM, (h + 1) * HEAD_DIM) for h in order])


def _pad_cols(w, n):
    return jnp.pad(w, ((0, 0), (0, n - w.shape[1])))


def _compress_weights(pe, w1, w2):
    G, D, H = NSA_KV_GROUPS, HEAD_DIM, CMP_HIDDEN
    half = CMP_LEN // 2
    pe_r = jnp.broadcast_to(pe.reshape(2, half, 1, D), (2, half, G, D)).reshape(2, half * G * D)
    eye = jnp.eye(G, dtype=w1.dtype)
    w1e = (w1.reshape(2, half, 1, D, 1, H) * eye[None, None, :, None, :, None]).reshape(2, half * G * D, G * H)
    w2e = (w2[None, :, None, :] * eye[:, None, :, None]).reshape(G * H, G * D)
    return pe_r, w1e.astype(BF16), w2e.astype(BF16)


def _layer_ab(x, c, norm_mix, norm_ffn, mod_w, mod_b, w_in, conv_w, conv_b, wa, ba, wx, bx, lam,
              pe_k, w1_k, w2_k, pe_v, w1_v, w2_v, w_out, wg, wu, wd):
    B, S, D = x.shape
    mod = _adaln(c, mod_w, mod_b).reshape(B, 6, D)
    C = LRU_WIDTH
    HQ = NSA_HEADS * HEAD_DIM
    GD = NSA_KV_GROUPS * HEAD_DIM
    order = _pair_heads(NSA_KV_GROUPS, NSA_HPG)
    o_q = 2 * C
    o_kv = o_q + HQ
    o_gt = o_kv + 6 * GD
    w_q = w_in[:, o_q:o_kv][:, _head_cols(order)]
    w_all = jnp.concatenate([w_in[:, :o_q], w_q, w_in[:, o_kv:o_gt], _pad_cols(w_in[:, o_gt:], LANES)],
                            axis=1).astype(BF16)
    ag, ax, q, kcmp, vcmp, kv4, gts = _inproj(
        x, mod, norm_mix, w_all,
        [(C, F32), (C, F32), (HQ, BF16), (GD, F32), (GD, F32), (4 * GD, BF16), (LANES, F32)])
    y_a = _lru(ag, ax, conv_w, conv_b, _block_diag(wa).astype(BF16), ba, _block_diag(wx).astype(BF16), bx, lam)
    NC = S // CMP_STRIDE
    kc, vc = _compress(kcmp.reshape(B, NC, CMP_STRIDE * GD), vcmp.reshape(B, NC, CMP_STRIDE * GD),
                       *_compress_weights(pe_k, w1_k, w2_k), *_compress_weights(pe_v, w1_v, w2_v))
    y_b = _nsa(q, gts, kc, vc, kv4)
    wo_a = w_out[:C].astype(BF16)
    wo_b = w_out[C:][_head_cols(order)].astype(BF16)
    return _mix_ffn(x, mod, norm_ffn, [y_a, y_b], [wo_a, wo_b],
                    wg.astype(BF16), wu.astype(BF16), wd.astype(BF16))


def _layer_c(x, c, norm_mix, norm_ffn, mod_w, mod_b, w_in, w_out, wg, wu, wd, final_norm):
    B, S, D = x.shape
    mod = _adaln(c, mod_w, mod_b).reshape(B, 6, D)
    HQ = DSA_HEADS * HEAD_DIM
    GD = DSA_KV_HEADS * HEAD_DIM
    HI = IDX_HEADS * IDX_DIM
    order = _pair_heads(DSA_KV_HEADS, DSA_HPG)
    o_k = HQ
    o_v = o_k + GD
    o_qi = o_v + GD
    o_ki = o_qi + HI
    o_wi = o_ki + IDX_DIM
    w_ki = w_in[:, o_ki:o_wi]
    w_all = jnp.concatenate([w_in[:, :HQ][:, _head_cols(order)], w_in[:, o_k:o_qi], w_in[:, o_qi:o_ki],
                             w_ki, w_ki, _pad_cols(w_in[:, o_wi:], LANES)], axis=1).astype(BF16)
    q, k, v, qi, ki, wi = _inproj(
        x, mod, norm_mix, w_all,
        [(HQ, BF16), (GD, BF16), (GD, BF16), (HI, BF16), (LANES, BF16), (LANES, F32)])
    y = _dsa(q, k, v, qi, ki, wi)
    wo = w_out[_head_cols(order)].astype(BF16)
    return _mix_ffn(x, mod, norm_ffn, [y], [wo], wg.astype(BF16), wu.astype(BF16), wd.astype(BF16),
                    g_final=final_norm)


def kernel(x, c, l0_norm_mix, l0_norm_ffn, l0_mod_w, l0_mod_b, l0_w_in, l0_conv_w, l0_conv_b, l0_lru_wa, l0_lru_ba, l0_lru_wx, l0_lru_bx, l0_lru_lambda, l0_cmp_pe_k, l0_cmp_w1_k, l0_cmp_w2_k, l0_cmp_pe_v, l0_cmp_w1_v, l0_cmp_w2_v, l0_w_out, l0_ffn_wg, l0_ffn_wu, l0_ffn_wd, l1_norm_mix, l1_norm_ffn, l1_mod_w, l1_mod_b, l1_w_in, l1_w_out, l1_ffn_wg, l1_ffn_wu, l1_ffn_wd, final_norm):
    x = _layer_ab(x, c, l0_norm_mix, l0_norm_ffn, l0_mod_w, l0_mod_b, l0_w_in, l0_conv_w, l0_conv_b,
                  l0_lru_wa, l0_lru_ba, l0_lru_wx, l0_lru_bx, l0_lru_lambda,
                  l0_cmp_pe_k, l0_cmp_w1_k, l0_cmp_w2_k, l0_cmp_pe_v, l0_cmp_w1_v, l0_cmp_w2_v,
                  l0_w_out, l0_ffn_wg, l0_ffn_wu, l0_ffn_wd)
    return _layer_c(x, c, l1_norm_mix, l1_norm_ffn, l1_mod_w, l1_mod_b, l1_w_in, l1_w_out,
                    l1_ffn_wg, l1_ffn_wu, l1_ffn_wd, final_norm)
```

```python
import functools

import numpy as np
import jax
import jax.numpy as jnp
from jax import lax
from jax.experimental import pallas as pl
from jax.experimental.pallas import tpu as pltpu

F32 = jnp.float32
BF16 = jnp.bfloat16

EPS = 1e-6
HEAD_DIM = 64
Q_BLOCK = 128
ATT_SUB = 2
LRU_WIDTH = 512
LRU_BLOCKS = 8
CONV_WIDTH = 4
LRU_C = 8.0
NSA_HEADS = 8
NSA_KV_GROUPS = 2
NSA_HPG = NSA_HEADS // NSA_KV_GROUPS
CMP_LEN = 32
CMP_STRIDE = 16
CMP_HIDDEN = 128
SLC_LEN = 64
SLC_TOPN = 8
WIN = 512
FORCE_SCORE = 1e4
DSA_HEADS = 16
DSA_KV_HEADS = 2
DSA_HPG = DSA_HEADS // DSA_KV_HEADS
IDX_HEADS = 8
IDX_DIM = 64
IDX_TOPK_MAX = 256

LANES = 128
NEG = -1e30
VMEM_LIMIT = 56 * 1024 * 1024
QK_SCALE = HEAD_DIM ** -0.5 * float(np.log2(np.e))
INT_MIN = -2147483648
KEY_NEG_INF = int(np.array(0xFF800000 ^ 0x7FFFFFFF, np.uint32).astype(np.int32))


def _cparams(n_grid):
    return pltpu.CompilerParams(dimension_semantics=("arbitrary",) * n_grid,
                                vmem_limit_bytes=VMEM_LIMIT)


def _const_spec(shape):
    nd = len(shape)
    return pl.BlockSpec(shape, lambda *_: (0,) * nd, pipeline_mode=pl.Buffered(1))


def _dot(a, b):
    return jnp.dot(a, b, preferred_element_type=F32)


def _dot_nt(a, b):
    return lax.dot_general(a, b, (((1,), (1,)), ((), ())), preferred_element_type=F32)


def _rms(x, g):
    return x * lax.rsqrt(jnp.mean(x * x, axis=-1, keepdims=True) + EPS) * g


def _adaln_kernel(c_ref, w_ref, b_ref, o_ref):
    c = c_ref[...]
    a = (c * jax.nn.sigmoid(c)).astype(BF16)
    o_ref[...] = _dot(a, w_ref[...].astype(BF16)) + b_ref[...]


def _adaln(c, w, b):
    B, D = c.shape
    N = w.shape[1]
    tn = N // 4
    return pl.pallas_call(
        _adaln_kernel,
        grid=(N // tn,),
        in_specs=[pl.BlockSpec((B, D), lambda j: (0, 0)),
                  pl.BlockSpec((D, tn), lambda j: (0, j)),
                  pl.BlockSpec((1, tn), lambda j: (0, j))],
        out_specs=pl.BlockSpec((B, tn), lambda j: (0, j)),
        out_shape=jax.ShapeDtypeStruct((B, N), F32),
        compiler_params=_cparams(1),
        name="adaln",
    )(c, w, b.reshape(1, N))


def _inproj_kernel(x_ref, mod_ref, g_ref, w_ref, *o_refs, splits):
    x = x_ref[0]
    h = _rms(x, g_ref[...]) * (1.0 + mod_ref[0, 1:2, :]) + mod_ref[0, 0:1, :]
    acc = _dot(h.astype(BF16), w_ref[...])
    for o_ref, (c0, wd) in zip(o_refs, splits):
        o_ref[0] = acc[:, c0:c0 + wd].astype(o_ref.dtype)


def _inproj(x, mod, g, w, outs, tm=512):
    B, S, D = x.shape
    N = w.shape[1]
    splits, c0 = [], 0
    for wd, _ in outs:
        splits.append((c0, wd))
        c0 += wd
    assert c0 == N
    return pl.pallas_call(
        functools.partial(_inproj_kernel, splits=tuple(splits)),
        grid=(B, S // tm),
        in_specs=[pl.BlockSpec((1, tm, D), lambda b, i: (b, i, 0)),
                  pl.BlockSpec((1, 6, D), lambda b, i: (b, 0, 0)),
                  _const_spec((1, D)),
                  _const_spec((D, N))],
        out_specs=[pl.BlockSpec((1, tm, wd), lambda b, i: (b, i, 0)) for wd, _ in outs],
        out_shape=[jax.ShapeDtypeStruct((B, S, wd), dt) for wd, dt in outs],
        compiler_params=_cparams(2),
        name="inproj",
    )(x, mod, g.reshape(1, D), w)


def _lru_kernel(ag_ref, ax_ref, cw_ref, cb_ref, wa_ref, ba_ref, wx_ref, bx_ref, lam_ref, o_ref,
                xpad, a_s, u_s, *, S, C, TC):
    PAD = 8
    xpad[0:PAD, :] = jnp.zeros((PAD, C), F32)
    xpad[PAD:PAD + S, :] = ax_ref[0]
    z = -lam_ref[...]
    sp = jnp.maximum(z, 0.0) + jnp.log(1.0 + jnp.exp(-jnp.abs(z)))
    for ci in range(S // TC):
        r0 = ci * TC
        xc = cb_ref[...]
        for k in range(CONV_WIDTH):
            off = PAD - (CONV_WIDTH - 1) + k + r0
            xc = xc + xpad[off:off + TC, :] * cw_ref[k:k + 1, :]
        xb = xc.astype(BF16)
        r = jax.nn.sigmoid(_dot(xb, wa_ref[...]) + ba_ref[...])
        i = jax.nn.sigmoid(_dot(xb, wx_ref[...]) + bx_ref[...])
        log_a = -LRU_C * r * sp
        a = jnp.exp(log_a)
        mult = jnp.sqrt(1.0 - jnp.exp(2.0 * log_a))
        if ci == 0:
            row = lax.broadcasted_iota(jnp.int32, (TC, C), 0)
            mult = jnp.where(row == 0, 1.0, mult)
        a_s[r0:r0 + TC, :] = a
        u_s[r0:r0 + TC, :] = mult * (i * xc)

    row8 = lax.broadcasted_iota(jnp.int32, (8, C), 0)

    def tile(ti, h):
        t0 = pl.multiple_of(ti * 8, 8)
        A = a_s[pl.ds(t0, 8), :]
        U = u_s[pl.ds(t0, 8), :]
        for d in (1, 2, 4):
            As = pltpu.roll(A, d, 0)
            Us = pltpu.roll(U, d, 0)
            ok = row8 >= d
            U = jnp.where(ok, A * Us + U, U)
            A = jnp.where(ok, A * As, A)
        H = U + A * h
        u_s[pl.ds(t0, 8), :] = H
        return H[7:8, :]

    lax.fori_loop(0, S // 8, tile, jnp.zeros((1, C), F32))

    for ci in range(S // TC):
        r0 = ci * TC
        o_ref[0, r0:r0 + TC, :] = (u_s[r0:r0 + TC, :]
                                   * jax.nn.gelu(ag_ref[0, r0:r0 + TC, :])).astype(o_ref.dtype)


def _lru(ag, ax, cw, cb, wa_bd, ba, wx_bd, bx, lam):
    B, S, C = ag.shape
    TC = 256
    row = lambda v: v.reshape(1, C)
    return pl.pallas_call(
        functools.partial(_lru_kernel, S=S, C=C, TC=TC),
        grid=(B,),
        in_specs=[pl.BlockSpec((1, S, C), lambda b: (b, 0, 0)),
                  pl.BlockSpec((1, S, C), lambda b: (b, 0, 0)),
                  _const_spec((CONV_WIDTH, C)), _const_spec((1, C)),
                  _const_spec((C, C)), _const_spec((1, C)),
                  _const_spec((C, C)), _const_spec((1, C)), _const_spec((1, C))],
        out_specs=pl.BlockSpec((1, S, C), lambda b: (b, 0, 0)),
        out_shape=jax.ShapeDtypeStruct((B, S, C), BF16),
        scratch_shapes=[pltpu.VMEM((S + 8, C), F32), pltpu.VMEM((S, C), F32), pltpu.VMEM((S, C), F32)],
        compiler_params=_cparams(1),
        name="rglru",
    )(ag, ax, cw, row(cb), wa_bd, row(ba), wx_bd, row(bx), row(lam))


def _compress_kernel(xk_ref, xv_ref, pek_ref, w1k_ref, w2k_ref, pev_ref, w1v_ref, w2v_ref,
                     ok_ref, ov_ref):
    def one(x_ref, pe_ref, w1_ref, w2_ref, o_ref):
        x = x_ref[0]
        p0 = _dot((x + pe_ref[0:1, :]).astype(BF16), w1_ref[0])
        p1 = _dot((x + pe_ref[1:2, :]).astype(BF16), w1_ref[1])
        nc = x.shape[0]
        hid = p0 + pltpu.roll(p1, nc - 1, 0)
        o_ref[0] = _dot(jax.nn.gelu(hid).astype(BF16), w2_ref[...]).astype(o_ref.dtype)

    one(xk_ref, pek_ref, w1k_ref, w2k_ref, ok_ref)
    one(xv_ref, pev_ref, w1v_ref, w2v_ref, ov_ref)


def _compress(xk, xv, pek, w1k, w2k, pev, w1v, w2v):
    B, NC, W = xk.shape
    GH = NSA_KV_GROUPS * CMP_HIDDEN
    GD = NSA_KV_GROUPS * HEAD_DIM
    xspec = pl.BlockSpec((1, NC, W), lambda b: (b, 0, 0))
    ospec = pl.BlockSpec((1, NC, GD), lambda b: (b, 0, 0))
    return pl.pallas_call(
        _compress_kernel,
        grid=(B,),
        in_specs=[xspec, xspec,
                  _const_spec((2, W)), _const_spec((2, W, GH)), _const_spec((GH, GD)),
                  _const_spec((2, W)), _const_spec((2, W, GH)), _const_spec((GH, GD))],
        out_specs=[ospec, ospec],
        out_shape=[jax.ShapeDtypeStruct((B, NC, GD), BF16)] * 2,
        compiler_params=_cparams(1),
        name="nsa_compress",
    )(xk, xv, pek, w1k, w2k, pev, w1v, w2v)


def _causal_buckets(S, step):
    step = min(S, step)
    assert S % step == 0
    return tuple(range(step, S + 1, step))


def _for_bucket(needed, buckets, body):
    lo = 0
    for lk in buckets:
        pl.when((needed > lo) & (needed <= lk))(functools.partial(body, lk))
        lo = lk


def _eye(QB):
    return (lax.broadcasted_iota(jnp.int32, (QB, LANES), 0)
            == lax.broadcasted_iota(jnp.int32, (QB, LANES), 1)).astype(BF16)


def _q_rows(slot, keep, eye):
    return jnp.concatenate([jnp.where(keep, slot, jnp.zeros((), BF16)), eye], axis=1)


def _colsum(x):
    n, q = x.shape
    part = jnp.sum(x.reshape(n // 64, 64, q), axis=0)
    return jnp.sum(part, axis=0, keepdims=True)


def _attend(qa, kaug, vaug):
    s = _dot_nt(qa, kaug)
    m = jnp.max(s, axis=-1, keepdims=True)
    oa = _dot(jnp.exp2(s - m).astype(BF16), vaug)
    return oa[:, 0:LANES], oa[:, LANES:2 * LANES]


def _nsa_body(LK, sub, q_ref, gt_ref, kc_ref, vc_ref, kv_ref, ovt_ref, expt_ref, o_ref, *, QB, NC, NSEL, NTOP):
    t0 = (pl.program_id(0) * ATT_SUB + sub) * QB
    rows = slice(sub * QB, (sub + 1) * QB)
    R = NSA_HPG
    G = NSA_KV_GROUPS
    low = lax.broadcasted_iota(jnp.int32, (QB, LANES), 1) < HEAD_DIM
    keep_g = (low, jnp.logical_not(low))
    eye = _eye(QB)
    q = q_ref[0, rows, :] * jnp.asarray(QK_SCALE, BF16)
    gates = jax.nn.sigmoid(gt_ref[0, rows, :])
    qa_g = [jnp.concatenate([_q_rows(q[:, r * LANES:(r + 1) * LANES], keep_g[g], eye) for r in range(R)], axis=0)
            for g in range(G)]
    qa = jnp.concatenate(qa_g, axis=0)

    def tq_t(n):
        return t0 + lax.broadcasted_iota(jnp.int32, (n, QB), 1)

    def vaug(v):
        return jnp.concatenate([v, jnp.ones(v.shape, BF16)], axis=1)

    cend = lax.broadcasted_iota(jnp.int32, (NC, QB), 0) * CMP_STRIDE + (CMP_LEN - 1)
    bias_c = jnp.where(cend <= tq_t(NC), 0.0, NEG).astype(BF16)
    s = _dot_nt(qa, jnp.concatenate([kc_ref[0], bias_c], axis=1))
    m = jnp.max(s, axis=-1, keepdims=True)
    p = jnp.exp2(s - m)
    inv = jnp.where(m > 0.5 * NEG, 1.0 / jnp.maximum(jnp.sum(p, axis=-1, keepdims=True), 1e-30), 0.0)
    o_cmp = _dot(p.astype(BF16), vc_ref[0]) * inv
    pn = p * inv

    start = pl.multiple_of(jnp.maximum(t0 - WIN, 0), QB)
    kpos_w = start + lax.broadcasted_iota(jnp.int32, (WIN + QB, QB), 0)
    bias_w = jnp.where((kpos_w <= tq_t(WIN + QB)) & (kpos_w > tq_t(WIN + QB) - WIN), 0.0, NEG).astype(BF16)
    kwin = kv_ref[0, pl.ds(start, WIN + QB), 2 * LANES:3 * LANES]
    vwin = kv_ref[0, pl.ds(start, WIN + QB), 3 * LANES:4 * LANES]
    ow, lw = _attend(qa, jnp.concatenate([kwin, bias_w], axis=1), vaug(vwin))
    o_win = ow / lw

    jrow = lax.broadcasted_iota(jnp.int32, (LANES, QB), 0)
    cur = lax.shift_right_logical(tq_t(LANES), 6)
    forced = (jrow == 0) | (jrow == cur) | (jrow == cur - 1)
    causal_s = lax.broadcasted_iota(jnp.int32, (LK, QB), 0) <= tq_t(LK)
    kslc = kv_ref[0, 0:LK, 0:LANES]
    vslc = vaug(kv_ref[0, 0:LK, LANES:2 * LANES])
    o_slc = []
    for g in range(G):
        base = g * R * QB
        psum = pn[base:base + QB]
        for r in range(1, R):
            psum = psum + pn[base + r * QB:base + (r + 1) * QB]
        hi = psum.astype(BF16)
        lo = (psum - hi.astype(F32)).astype(BF16)
        imp = _dot_nt(ovt_ref[...], hi) + _dot_nt(ovt_ref[...], lo)
        imp = jnp.where(forced, FORCE_SCORE, imp)
        imp = jnp.where(jrow <= cur, imp, -jnp.inf)
        imp = imp[0:NSEL]
        jr = jrow[0:NSEL]
        cnt = jnp.zeros((NSEL, QB), F32)
        for j2 in range(NSEL):
            rowv = imp[j2:j2 + 1, :]
            cnt = cnt + jnp.where(jr > j2, jnp.where(rowv >= imp, 1.0, 0.0), jnp.where(rowv > imp, 1.0, 0.0))
        sel_t = jnp.where(cnt < NTOP, 1.0, 0.0)
        if NSEL < LANES:
            sel_t = jnp.concatenate([sel_t, jnp.zeros((LANES - NSEL, QB), F32)], axis=0)
        picked = _dot(expt_ref[0:LK, :], sel_t.astype(BF16))
        bias_s = jnp.where((picked > 0.5) & causal_s, 0.0, NEG).astype(BF16)
        os_, ls = _attend(qa_g[g], jnp.concatenate([kslc, bias_s], axis=1), vslc)
        o_slc.append(os_ / ls)

    for r in range(R):
        acc = jnp.zeros((QB, LANES), F32)
        ra, rb = r * QB, (R + r) * QB
        branches = ((o_cmp[ra:ra + QB], o_cmp[rb:rb + QB]),
                    (o_slc[0][ra:ra + QB], o_slc[1][ra:ra + QB]),
                    (o_win[ra:ra + QB], o_win[rb:rb + QB]))
        for j, (va, vb) in enumerate(branches):
            ca, cb = 3 * r + j, 3 * (R + r) + j
            gate = jnp.where(low, gates[:, ca:ca + 1], gates[:, cb:cb + 1])
            acc = acc + gate * jnp.where(low, va, vb)
        o_ref[0, rows, r * LANES:(r + 1) * LANES] = acc.astype(o_ref.dtype)


def _nsa_kernel(*refs, QB, buckets, **kw):
    needed = (pl.program_id(0) + 1) * ATT_SUB * QB

    def body(lk):
        for sub in range(ATT_SUB):
            _nsa_body(lk, sub, *refs, QB=QB, **kw)

    _for_bucket(needed, buckets, body)


def _nsa(q, gts, kc, vc, kv4):
    B, S, _ = q.shape
    QB = Q_BLOCK
    NC = kc.shape[1]
    NSEL = S // SLC_LEN
    NTOP = min(SLC_TOPN, NSEL)
    TQ = ATT_SUB * QB
    assert NSEL <= LANES and S % TQ == 0 and S >= WIN + QB
    c = np.arange(NC)[None, :] * CMP_STRIDE
    j = np.arange(LANES)[:, None] * SLC_LEN
    valid_c = np.arange(NC)[None, :] < (S - CMP_LEN) // CMP_STRIDE + 1
    ovt = ((c < j + SLC_LEN) & (c + CMP_LEN > j) & valid_c & (np.arange(LANES)[:, None] < NSEL))
    expand_t = (np.arange(S)[:, None] // SLC_LEN) == np.arange(LANES)[None, :]
    GD = NSA_KV_GROUPS * HEAD_DIM
    return pl.pallas_call(
        functools.partial(_nsa_kernel, QB=QB, NC=NC, NSEL=NSEL, NTOP=NTOP, buckets=_causal_buckets(S, 512)),
        grid=(S // TQ, B),
        in_specs=[pl.BlockSpec((1, TQ, NSA_HPG * LANES), lambda i, b: (b, i, 0)),
                  pl.BlockSpec((1, TQ, LANES), lambda i, b: (b, i, 0)),
                  pl.BlockSpec((1, NC, GD), lambda i, b: (b, 0, 0)),
                  pl.BlockSpec((1, NC, GD), lambda i, b: (b, 0, 0)),
                  pl.BlockSpec((1, S, 4 * GD), lambda i, b: (b, 0, 0)),
                  _const_spec((LANES, NC)), _const_spec((S, LANES))],
        out_specs=pl.BlockSpec((1, TQ, NSA_HPG * LANES), lambda i, b: (b, i, 0)),
        out_shape=jax.ShapeDtypeStruct((B, S, NSA_HPG * LANES), BF16),
        compiler_params=_cparams(2),
        name="nsa_attention",
    )(q, gts, kc, vc, kv4, jnp.asarray(ovt, BF16), jnp.asarray(expand_t, BF16))


def _dsa_body(LK, q_ref, k_ref, v_ref, qi_ref, ki_ref, wi_ref, o_ref, *, QB, NKEEP, CH):
    TQ = ATT_SUB * QB
    t0 = pl.program_id(0) * TQ
    low_t = lax.broadcasted_iota(jnp.int32, (TQ, LANES), 1) < HEAD_DIM
    zero = jnp.zeros((), BF16)
    causal = (lax.broadcasted_iota(jnp.int32, (LK, TQ), 0)
              <= t0 + lax.broadcasted_iota(jnp.int32, (LK, TQ), 1))

    w_t = (wi_ref[0] * (IDX_DIM ** -0.5 * IDX_HEADS ** -0.5)).T
    qi = qi_ref[0]
    ki = ki_ref[0, 0:LK, :]
    score = None
    for u in range(IDX_HEADS // 2):
        slot = qi[:, u * LANES:(u + 1) * LANES]
        pair = jnp.concatenate([jnp.where(low_t, slot, zero), jnp.where(low_t, zero, slot)], axis=0)
        sc = jnp.maximum(_dot_nt(ki, pair), 0.0)
        term = sc[:, 0:TQ] * w_t[2 * u:2 * u + 1, :] + sc[:, TQ:2 * TQ] * w_t[2 * u + 1:2 * u + 2, :]
        score = term if score is None else score + term

    bits = pltpu.bitcast(score, jnp.int32)
    key = bits ^ (lax.shift_right_arithmetic(bits, 31) & 0x7FFFFFFF)
    key = jnp.where(causal, key, KEY_NEG_INF)

    def bit_step(i, t_u):
        c_u = t_u | lax.shift_left(jnp.int32(1), 31 - i)
        cnt = _colsum(jnp.where(key >= (c_u ^ INT_MIN), 1.0, 0.0))
        return jnp.where(cnt >= NKEEP, c_u, t_u)

    thr = lax.fori_loop(0, 32, bit_step, jnp.zeros((1, TQ), jnp.int32)) ^ INT_MIN
    gt = key > thr
    eq = key == thr
    need = NKEEP - _colsum(jnp.where(gt, 1.0, 0.0))
    eq_b = jnp.where(eq, 1.0, 0.0).astype(BF16)
    tri = (lax.broadcasted_iota(jnp.int32, (CH, CH), 0)
           > lax.broadcasted_iota(jnp.int32, (CH, CH), 1)).astype(BF16)
    ranks, before = [], jnp.zeros((1, TQ), F32)
    for c in range(LK // CH):
        e = eq_b[c * CH:(c + 1) * CH]
        ranks.append(_dot(tri, e) + before)
        before = before + _colsum(e.astype(F32))
    rank = jnp.concatenate(ranks, axis=0)
    bias = jnp.where((gt | (eq & (rank < need))) & causal, 0.0, NEG).astype(BF16)
    k = k_ref[0, 0:LK, :]
    vaug = jnp.concatenate([v_ref[0, 0:LK, :], jnp.ones((LK, LANES), BF16)], axis=1)

    low = lax.broadcasted_iota(jnp.int32, (QB, LANES), 1) < HEAD_DIM
    not_low = jnp.logical_not(low)
    eye = _eye(QB)
    scale = jnp.asarray(QK_SCALE, BF16)
    HALF = DSA_HPG // 2
    for sub in range(ATT_SUB):
        rows_q = slice(sub * QB, (sub + 1) * QB)
        kaug = jnp.concatenate([k, bias[:, sub * QB:(sub + 1) * QB]], axis=1)
        for half in range(2):
            rows = []
            for r in range(half * HALF, (half + 1) * HALF):
                slot = q_ref[0, rows_q, r * LANES:(r + 1) * LANES] * scale
                rows += [_q_rows(slot, low, eye), _q_rows(slot, not_low, eye)]
            o, l = _attend(jnp.concatenate(rows, axis=0), kaug, vaug)
            o = o / l
            for i in range(HALF):
                r = half * HALF + i
                o_ref[0, rows_q, r * LANES:(r + 1) * LANES] = jnp.where(
                    low, o[2 * i * QB:(2 * i + 1) * QB], o[(2 * i + 1) * QB:(2 * i + 2) * QB]).astype(o_ref.dtype)


def _dsa_kernel(*refs, QB, buckets, **kw):
    needed = (pl.program_id(0) + 1) * ATT_SUB * QB
    _for_bucket(needed, buckets, lambda lk: _dsa_body(lk, *refs, QB=QB, **kw))


def _dsa(q, k, v, qi, ki, wi):
    B, S, _ = q.shape
    QB = Q_BLOCK
    NKEEP = min(IDX_TOPK_MAX, S // 4)
    GD = DSA_KV_HEADS * HEAD_DIM
    TQ = ATT_SUB * QB
    assert S % TQ == 0
    blk = lambda w: pl.BlockSpec((1, TQ, w), lambda i, b: (b, i, 0))
    full = lambda w: pl.BlockSpec((1, S, w), lambda i, b: (b, 0, 0))
    return pl.pallas_call(
        functools.partial(_dsa_kernel, QB=QB, NKEEP=NKEEP, CH=256, buckets=_causal_buckets(S, 512)),
        grid=(S // TQ, B),
        in_specs=[blk(DSA_HPG * LANES), full(GD), full(GD), blk(IDX_HEADS * IDX_DIM), full(LANES), blk(LANES)],
        out_specs=blk(DSA_HPG * LANES),
        out_shape=jax.ShapeDtypeStruct((B, S, DSA_HPG * LANES), BF16),
        compiler_params=_cparams(2),
        name="dsa_attention",
    )(q, k, v, qi, ki, wi)


def _mix_ffn_kernel(*refs, n_y, final):
    x_ref, mod_ref, gf_ref = refs[0:3]
    y_refs = refs[3:3 + n_y]
    wo_refs = refs[3 + n_y:3 + 2 * n_y]
    wg_ref, wu_ref, wd_ref = refs[3 + 2 * n_y:6 + 2 * n_y]
    rest = refs[6 + 2 * n_y:]
    o_ref = rest[-1]
    mix = None
    for y_ref, wo_ref in zip(y_refs, wo_refs):
        t = _dot(y_ref[0], wo_ref[...])
        mix = t if mix is None else mix + t
    x1 = x_ref[0] + mod_ref[0, 2:3, :] * mix
    h = (_rms(x1, gf_ref[...]) * (1.0 + mod_ref[0, 4:5, :]) + mod_ref[0, 3:4, :]).astype(BF16)
    gate = _dot(h, wg_ref[...])
    up = _dot(h, wu_ref[...])
    act = (gate * jax.nn.sigmoid(gate) * up).astype(BF16)
    x2 = x1 + mod_ref[0, 5:6, :] * _dot(act, wd_ref[...])
    if final:
        x2 = _rms(x2, rest[0][...])
    o_ref[0] = x2


def _mix_ffn(x, mod, g_ffn, ys, wos, wg, wu, wd, g_final=None, tm=512):
    B, S, D = x.shape
    FF = wg.shape[1]
    final = g_final is not None
    row_blk = lambda w: pl.BlockSpec((1, tm, w), lambda b, i: (b, i, 0))
    in_specs = ([row_blk(D), pl.BlockSpec((1, 6, D), lambda b, i: (b, 0, 0)), _const_spec((1, D))]
                + [row_blk(y.shape[2]) for y in ys]
                + [_const_spec(w.shape) for w in wos]
                + [_const_spec((D, FF)), _const_spec((D, FF)), _const_spec((FF, D))])
    args = [x, mod, g_ffn.reshape(1, D), *ys, *wos, wg, wu, wd]
    if final:
        in_specs.append(_const_spec((1, D)))
        args.append(g_final.reshape(1, D))
    return pl.pallas_call(
        functools.partial(_mix_ffn_kernel, n_y=len(ys), final=final),
        grid=(B, S // tm),
        in_specs=in_specs,
        out_specs=row_blk(D),
        out_shape=jax.ShapeDtypeStruct((B, S, D), F32),
        compiler_params=_cparams(2),
        name="mix_ffn",
    )(*args)


def _block_diag(w):
    nb, bi, bo = w.shape
    eye = jnp.eye(nb, dtype=w.dtype)
    return (w[:, :, None, :] * eye[:, None, :, None]).reshape(nb * bi, nb * bo)


def _pair_heads(n_groups, per_group):
    assert n_groups == 2
    return [g * per_group + r for r in range(per_group) for g in range(n_groups)]


def _head_cols(order):
    return np.concatenate([np.arange(h * HEAD_DIM, (h + 1) * HEAD_DIM) for h in order])


def _pad_cols(w, n):
    return jnp.pad(w, ((0, 0), (0, n - w.shape[1])))


def _compress_weights(pe, w1, w2):
    G, D, H = NSA_KV_GROUPS, HEAD_DIM, CMP_HIDDEN
    half = CMP_LEN // 2
    pe_r = jnp.broadcast_to(pe.reshape(2, half, 1, D), (2, half, G, D)).reshape(2, half * G * D)
    eye = jnp.eye(G, dtype=w1.dtype)
    w1e = (w1.reshape(2, half, 1, D, 1, H) * eye[None, None, :, None, :, None]).reshape(2, half * G * D, G * H)
    w2e = (w2[None, :, None, :] * eye[:, None, :, None]).reshape(G * H, G * D)
    return pe_r, w1e.astype(BF16), w2e.astype(BF16)


def _layer_ab(x, c, norm_mix, norm_ffn, mod_w, mod_b, w_in, conv_w, conv_b, wa, ba, wx, bx, lam,
              pe_k, w1_k, w2_k, pe_v, w1_v, w2_v, w_out, wg, wu, wd):
    B, S, D = x.shape
    mod = _adaln(c, mod_w, mod_b).reshape(B, 6, D)
    C = LRU_WIDTH
    HQ = NSA_HEADS * HEAD_DIM
    GD = NSA_KV_GROUPS * HEAD_DIM
    order = _pair_heads(NSA_KV_GROUPS, NSA_HPG)
    o_q = 2 * C
    o_kv = o_q + HQ
    o_gt = o_kv + 6 * GD
    w_q = w_in[:, o_q:o_kv][:, _head_cols(order)]
    w_all = jnp.concatenate([w_in[:, :o_q], w_q, w_in[:, o_kv:o_gt], _pad_cols(w_in[:, o_gt:], LANES)],
                            axis=1).astype(BF16)
    ag, ax, q, kcmp, vcmp, kv4, gts = _inproj(
        x, mod, norm_mix, w_all,
        [(C, F32), (C, F32), (HQ, BF16), (GD, F32), (GD, F32), (4 * GD, BF16), (LANES, F32)])
    y_a = _lru(ag, ax, conv_w, conv_b, _block_diag(wa).astype(BF16), ba, _block_diag(wx).astype(BF16), bx, lam)
    NC = S // CMP_STRIDE
    kc, vc = _compress(kcmp.reshape(B, NC, CMP_STRIDE * GD), vcmp.reshape(B, NC, CMP_STRIDE * GD),
                       *_compress_weights(pe_k, w1_k, w2_k), *_compress_weights(pe_v, w1_v, w2_v))
    y_b = _nsa(q, gts, kc, vc, kv4)
    wo_a = w_out[:C].astype(BF16)
    wo_b = w_out[C:][_head_cols(order)].astype(BF16)
    return _mix_ffn(x, mod, norm_ffn, [y_a, y_b], [wo_a, wo_b],
                    wg.astype(BF16), wu.astype(BF16), wd.astype(BF16))


def _layer_c(x, c, norm_mix, norm_ffn, mod_w, mod_b, w_in, w_out, wg, wu, wd, final_norm):
    B, S, D = x.shape
    mod = _adaln(c, mod_w, mod_b).reshape(B, 6, D)
    HQ = DSA_HEADS * HEAD_DIM
    GD = DSA_KV_HEADS * HEAD_DIM
    HI = IDX_HEADS * IDX_DIM
    order = _pair_heads(DSA_KV_HEADS, DSA_HPG)
    o_k = HQ
    o_v = o_k + GD
    o_qi = o_v + GD
    o_ki = o_qi + HI
    o_wi = o_ki + IDX_DIM
    w_ki = w_in[:, o_ki:o_wi]
    w_all = jnp.concatenate([w_in[:, :HQ][:, _head_cols(order)], w_in[:, o_k:o_qi], w_in[:, o_qi:o_ki],
                             w_ki, w_ki, _pad_cols(w_in[:, o_wi:], LANES)], axis=1).astype(BF16)
    q, k, v, qi, ki, wi = _inproj(
        x, mod, norm_mix, w_all,
        [(HQ, BF16), (GD, BF16), (GD, BF16), (HI, BF16), (LANES, BF16), (LANES, F32)])
    y = _dsa(q, k, v, qi, ki, wi)
    wo = w_out[_head_cols(order)].astype(BF16)
    return _mix_ffn(x, mod, norm_ffn, [y], [wo], wg.astype(BF16), wu.astype(BF16), wd.astype(BF16),
                    g_final=final_norm)


def kernel(x, c, l0_norm_mix, l0_norm_ffn, l0_mod_w, l0_mod_b, l0_w_in, l0_conv_w, l0_conv_b, l0_lru_wa, l0_lru_ba, l0_lru_wx, l0_lru_bx, l0_lru_lambda, l0_cmp_pe_k, l0_cmp_w1_k, l0_cmp_w2_k, l0_cmp_pe_v, l0_cmp_w1_v, l0_cmp_w2_v, l0_w_out, l0_ffn_wg, l0_ffn_wu, l0_ffn_wd, l1_norm_mix, l1_norm_ffn, l1_mod_w, l1_mod_b, l1_w_in, l1_w_out, l1_ffn_wg, l1_ffn_wu, l1_ffn_wd, final_norm):
    x = _layer_ab(x, c, l0_norm_mix, l0_norm_ffn, l0_mod_w, l0_mod_b, l0_w_in, l0_conv_w, l0_conv_b,
                  l0_lru_wa, l0_lru_ba, l0_lru_wx, l0_lru_bx, l0_lru_lambda,
                  l0_cmp_pe_k, l0_cmp_w1_k, l0_cmp_w2_k, l0_cmp_pe_v, l0_cmp_w1_v, l0_cmp_w2_v,
                  l0_w_out, l0_ffn_wg, l0_ffn_wu, l0_ffn_wd)
    return _layer_c(x, c, l1_norm_mix, l1_norm_ffn, l1_mod_w, l1_mod_b, l1_w_in, l1_w_out,
                    l1_ffn_wg, l1_ffn_wu, l1_ffn_wd, final_norm)
```

```python
import functools

import numpy as np
import jax
import jax.numpy as jnp
from jax import lax
from jax.experimental import pallas as pl
from jax.experimental.pallas import tpu as pltpu

F32 = jnp.float32
BF16 = jnp.bfloat16

EPS = 1e-6
HEAD_DIM = 64
Q_BLOCK = 128
ATT_SUB = 2
LRU_WIDTH = 512
LRU_BLOCKS = 8
CONV_WIDTH = 4
LRU_C = 8.0
NSA_HEADS = 8
NSA_KV_GROUPS = 2
NSA_HPG = NSA_HEADS // NSA_KV_GROUPS
CMP_LEN = 32
CMP_STRIDE = 16
CMP_HIDDEN = 128
SLC_LEN = 64
SLC_TOPN = 8
WIN = 512
FORCE_SCORE = 1e4
DSA_HEADS = 16
DSA_KV_HEADS = 2
DSA_HPG = DSA_HEADS // DSA_KV_HEADS
IDX_HEADS = 8
IDX_DIM = 64
IDX_TOPK_MAX = 256

LANES = 128
NEG = -1e30
VMEM_LIMIT = 56 * 1024 * 1024
QK_SCALE = HEAD_DIM ** -0.5 * float(np.log2(np.e))
INT_MIN = -2147483648
KEY_NEG_INF = int(np.array(0xFF800000 ^ 0x7FFFFFFF, np.uint32).astype(np.int32))


def _cparams(n_grid):
    return pltpu.CompilerParams(dimension_semantics=("arbitrary",) * n_grid,
                                vmem_limit_bytes=VMEM_LIMIT)


def _const_spec(shape):
    nd = len(shape)
    return pl.BlockSpec(shape, lambda *_: (0,) * nd, pipeline_mode=pl.Buffered(1))


def _dot(a, b):
    return jnp.dot(a, b, preferred_element_type=F32)


def _dot_nt(a, b):
    return lax.dot_general(a, b, (((1,), (1,)), ((), ())), preferred_element_type=F32)


def _rms(x, g):
    return x * lax.rsqrt(jnp.mean(x * x, axis=-1, keepdims=True) + EPS) * g


def _adaln_kernel(c_ref, w_ref, b_ref, o_ref):
    c = c_ref[...]
    a = (c * jax.nn.sigmoid(c)).astype(BF16)
    o_ref[...] = _dot(a, w_ref[...].astype(BF16)) + b_ref[...]


def _adaln(c, w, b):
    B, D = c.shape
    N = w.shape[1]
    tn = N // 4
    return pl.pallas_call(
        _adaln_kernel,
        grid=(N // tn,),
        in_specs=[pl.BlockSpec((B, D), lambda j: (0, 0)),
                  pl.BlockSpec((D, tn), lambda j: (0, j)),
                  pl.BlockSpec((1, tn), lambda j: (0, j))],
        out_specs=pl.BlockSpec((B, tn), lambda j: (0, j)),
        out_shape=jax.ShapeDtypeStruct((B, N), F32),
        compiler_params=_cparams(1),
        name="adaln",
    )(c, w, b.reshape(1, N))


def _inproj_kernel(x_ref, mod_ref, g_ref, w_ref, *o_refs, splits):
    x = x_ref[0]
    h = _rms(x, g_ref[...]) * (1.0 + mod_ref[0, 1:2, :]) + mod_ref[0, 0:1, :]
    acc = _dot(h.astype(BF16), w_ref[...])
    for o_ref, (c0, wd) in zip(o_refs, splits):
        o_ref[0] = acc[:, c0:c0 + wd].astype(o_ref.dtype)


def _inproj(x, mod, g, w, outs, tm=512):
    B, S, D = x.shape
    N = w.shape[1]
    splits, c0 = [], 0
    for wd, _ in outs:
        splits.append((c0, wd))
        c0 += wd
    assert c0 == N
    return pl.pallas_call(
        functools.partial(_inproj_kernel, splits=tuple(splits)),
        grid=(B, S // tm),
        in_specs=[pl.BlockSpec((1, tm, D), lambda b, i: (b, i, 0)),
                  pl.BlockSpec((1, 6, D), lambda b, i: (b, 0, 0)),
                  _const_spec((1, D)),
                  _const_spec((D, N))],
        out_specs=[pl.BlockSpec((1, tm, wd), lambda b, i: (b, i, 0)) for wd, _ in outs],
        out_shape=[jax.ShapeDtypeStruct((B, S, wd), dt) for wd, dt in outs],
        compiler_params=_cparams(2),
        name="inproj",
    )(x, mod, g.reshape(1, D), w)


def _lru_kernel(ag_ref, ax_ref, cw_ref, cb_ref, wa_ref, ba_ref, wx_ref, bx_ref, lam_ref, o_ref,
                xpad, a_s, u_s, *, S, C, TC):
    PAD = 8
    xpad[0:PAD, :] = jnp.zeros((PAD, C), F32)
    xpad[PAD:PAD + S, :] = ax_ref[0]
    z = -lam_ref[...]
    sp = jnp.maximum(z, 0.0) + jnp.log(1.0 + jnp.exp(-jnp.abs(z)))
    for ci in range(S // TC):
        r0 = ci * TC
        xc = cb_ref[...]
        for k in range(CONV_WIDTH):
            off = PAD - (CONV_WIDTH - 1) + k + r0
            xc = xc + xpad[off:off + TC, :] * cw_ref[k:k + 1, :]
        xb = xc.astype(BF16)
        r = jax.nn.sigmoid(_dot(xb, wa_ref[...]) + ba_ref[...])
        i = jax.nn.sigmoid(_dot(xb, wx_ref[...]) + bx_ref[...])
        log_a = -LRU_C * r * sp
        a = jnp.exp(log_a)
        mult = jnp.sqrt(1.0 - jnp.exp(2.0 * log_a))
        if ci == 0:
            row = lax.broadcasted_iota(jnp.int32, (TC, C), 0)
            mult = jnp.where(row == 0, 1.0, mult)
        a_s[r0:r0 + TC, :] = a
        u_s[r0:r0 + TC, :] = mult * (i * xc)

    row8 = lax.broadcasted_iota(jnp.int32, (8, C), 0)

    def tile(ti, h):
        t0 = pl.multiple_of(ti * 8, 8)
        A = a_s[pl.ds(t0, 8), :]
        U = u_s[pl.ds(t0, 8), :]
        for d in (1, 2, 4):
            As = pltpu.roll(A, d, 0)
            Us = pltpu.roll(U, d, 0)
            ok = row8 >= d
            U = jnp.where(ok, A * Us + U, U)
            A = jnp.where(ok, A * As, A)
        H = U + A * h
        u_s[pl.ds(t0, 8), :] = H
        return H[7:8, :]

    lax.fori_loop(0, S // 8, tile, jnp.zeros((1, C), F32))

    for ci in range(S // TC):
        r0 = ci * TC
        o_ref[0, r0:r0 + TC, :] = (u_s[r0:r0 + TC, :]
                                   * jax.nn.gelu(ag_ref[0, r0:r0 + TC, :])).astype(o_ref.dtype)


def _lru(ag, ax, cw, cb, wa_bd, ba, wx_bd, bx, lam):
    B, S, C = ag.shape
    TC = 256
    row = lambda v: v.reshape(1, C)
    return pl.pallas_call(
        functools.partial(_lru_kernel, S=S, C=C, TC=TC),
        grid=(B,),
        in_specs=[pl.BlockSpec((1, S, C), lambda b: (b, 0, 0)),
                  pl.BlockSpec((1, S, C), lambda b: (b, 0, 0)),
                  _const_spec((CONV_WIDTH, C)), _const_spec((1, C)),
                  _const_spec((C, C)), _const_spec((1, C)),
                  _const_spec((C, C)), _const_spec((1, C)), _const_spec((1, C))],
        out_specs=pl.BlockSpec((1, S, C), lambda b: (b, 0, 0)),
        out_shape=jax.ShapeDtypeStruct((B, S, C), BF16),
        scratch_shapes=[pltpu.VMEM((S + 8, C), F32), pltpu.VMEM((S, C), F32), pltpu.VMEM((S, C), F32)],
        compiler_params=_cparams(1),
        name="rglru",
    )(ag, ax, cw, row(cb), wa_bd, row(ba), wx_bd, row(bx), row(lam))


def _compress_kernel(xk_ref, xv_ref, pek_ref, w1k_ref, w2k_ref, pev_ref, w1v_ref, w2v_ref,
                     ok_ref, ov_ref):
    def one(x_ref, pe_ref, w1_ref, w2_ref, o_ref):
        x = x_ref[0]
        p0 = _dot((x + pe_ref[0:1, :]).astype(BF16), w1_ref[0])
        p1 = _dot((x + pe_ref[1:2, :]).astype(BF16), w1_ref[1])
        nc = x.shape[0]
        hid = p0 + pltpu.roll(p1, nc - 1, 0)
        o_ref[0] = _dot(jax.nn.gelu(hid).astype(BF16), w2_ref[...]).astype(o_ref.dtype)

    one(xk_ref, pek_ref, w1k_ref, w2k_ref, ok_ref)
    one(xv_ref, pev_ref, w1v_ref, w2v_ref, ov_ref)


def _compress(xk, xv, pek, w1k, w2k, pev, w1v, w2v):
    B, NC, W = xk.shape
    GH = NSA_KV_GROUPS * CMP_HIDDEN
    GD = NSA_KV_GROUPS * HEAD_DIM
    xspec = pl.BlockSpec((1, NC, W), lambda b: (b, 0, 0))
    ospec = pl.BlockSpec((1, NC, GD), lambda b: (b, 0, 0))
    return pl.pallas_call(
        _compress_kernel,
        grid=(B,),
        in_specs=[xspec, xspec,
                  _const_spec((2, W)), _const_spec((2, W, GH)), _const_spec((GH, GD)),
                  _const_spec((2, W)), _const_spec((2, W, GH)), _const_spec((GH, GD))],
        out_specs=[ospec, ospec],
        out_shape=[jax.ShapeDtypeStruct((B, NC, GD), BF16)] * 2,
        compiler_params=_cparams(1),
        name="nsa_compress",
    )(xk, xv, pek, w1k, w2k, pev, w1v, w2v)


def _causal_buckets(S, step):
    step = min(S, step)
    assert S % step == 0
    return tuple(range(step, S + 1, step))


def _coarse_buckets(S, align):
    up = lambda n: -(-n // align) * align
    return tuple(sorted({up(S // 4), up(5 * S // 8), S}))


def _for_bucket(needed, buckets, body):
    lo = 0
    for lk in buckets:
        pl.when((needed > lo) & (needed <= lk))(functools.partial(body, lk))
        lo = lk


def _eye(QB):
    return (lax.broadcasted_iota(jnp.int32, (QB, LANES), 0)
            == lax.broadcasted_iota(jnp.int32, (QB, LANES), 1)).astype(BF16)


def _q_rows(slot, keep, eye):
    return jnp.concatenate([jnp.where(keep, slot, jnp.zeros((), BF16)), eye], axis=1)


def _colsum(x):
    n, q = x.shape
    part = jnp.sum(x.reshape(n // 64, 64, q), axis=0)
    return jnp.sum(part, axis=0, keepdims=True)


def _attend(qa, kaug, vaug):
    s = _dot_nt(qa, kaug)
    m = jnp.max(s, axis=-1, keepdims=True)
    oa = _dot(jnp.exp2(s - m).astype(BF16), vaug)
    return oa[:, 0:LANES], oa[:, LANES:2 * LANES]


def _nsa_body(LK, sub, q_ref, gt_ref, kc_ref, vc_ref, kv_ref, ovt_ref, expt_ref, o_ref, *, QB, NC, NSEL, NTOP):
    t0 = (pl.program_id(0) * ATT_SUB + sub) * QB
    rows = slice(sub * QB, (sub + 1) * QB)
    R = NSA_HPG
    G = NSA_KV_GROUPS
    low = lax.broadcasted_iota(jnp.int32, (QB, LANES), 1) < HEAD_DIM
    keep_g = (low, jnp.logical_not(low))
    eye = _eye(QB)
    q = q_ref[0, rows, :] * jnp.asarray(QK_SCALE, BF16)
    gates = jax.nn.sigmoid(gt_ref[0, rows, :])
    qa_g = [jnp.concatenate([_q_rows(q[:, r * LANES:(r + 1) * LANES], keep_g[g], eye) for r in range(R)], axis=0)
            for g in range(G)]
    qa = jnp.concatenate(qa_g, axis=0)

    def tq_t(n):
        return t0 + lax.broadcasted_iota(jnp.int32, (n, QB), 1)

    def vaug(v):
        return jnp.concatenate([v, jnp.ones(v.shape, BF16)], axis=1)

    cend = lax.broadcasted_iota(jnp.int32, (NC, QB), 0) * CMP_STRIDE + (CMP_LEN - 1)
    bias_c = jnp.where(cend <= tq_t(NC), 0.0, NEG).astype(BF16)
    s = _dot_nt(qa, jnp.concatenate([kc_ref[0], bias_c], axis=1))
    m = jnp.max(s, axis=-1, keepdims=True)
    p = jnp.exp2(s - m)
    inv = jnp.where(m > 0.5 * NEG, 1.0 / jnp.maximum(jnp.sum(p, axis=-1, keepdims=True), 1e-30), 0.0)
    o_cmp = _dot(p.astype(BF16), vc_ref[0]) * inv
    pn = p * inv

    start = pl.multiple_of(jnp.maximum(t0 - WIN, 0), QB)
    kpos_w = start + lax.broadcasted_iota(jnp.int32, (WIN + QB, QB), 0)
    bias_w = jnp.where((kpos_w <= tq_t(WIN + QB)) & (kpos_w > tq_t(WIN + QB) - WIN), 0.0, NEG).astype(BF16)
    kwin = kv_ref[0, pl.ds(start, WIN + QB), 2 * LANES:3 * LANES]
    vwin = kv_ref[0, pl.ds(start, WIN + QB), 3 * LANES:4 * LANES]
    ow, lw = _attend(qa, jnp.concatenate([kwin, bias_w], axis=1), vaug(vwin))
    o_win = ow / lw

    jrow = lax.broadcasted_iota(jnp.int32, (LANES, QB), 0)
    cur = lax.shift_right_logical(tq_t(LANES), 6)
    forced = (jrow == 0) | (jrow == cur) | (jrow == cur - 1)
    causal_s = lax.broadcasted_iota(jnp.int32, (LK, QB), 0) <= tq_t(LK)
    kslc = kv_ref[0, 0:LK, 0:LANES]
    vslc = vaug(kv_ref[0, 0:LK, LANES:2 * LANES])
    o_slc = []
    for g in range(G):
        base = g * R * QB
        psum = pn[base:base + QB]
        for r in range(1, R):
            psum = psum + pn[base + r * QB:base + (r + 1) * QB]
        hi = psum.astype(BF16)
        lo = (psum - hi.astype(F32)).astype(BF16)
        imp = _dot_nt(ovt_ref[...], hi) + _dot_nt(ovt_ref[...], lo)
        imp = jnp.where(forced, FORCE_SCORE, imp)
        imp = jnp.where(jrow <= cur, imp, -jnp.inf)
        imp = imp[0:NSEL]
        jr = jrow[0:NSEL]
        cnt = jnp.zeros((NSEL, QB), F32)
        for j2 in range(NSEL):
            rowv = imp[j2:j2 + 1, :]
            cnt = cnt + jnp.where(jr > j2, jnp.where(rowv >= imp, 1.0, 0.0), jnp.where(rowv > imp, 1.0, 0.0))
        sel_t = jnp.where(cnt < NTOP, 1.0, 0.0)
        if NSEL < LANES:
            sel_t = jnp.concatenate([sel_t, jnp.zeros((LANES - NSEL, QB), F32)], axis=0)
        picked = _dot(expt_ref[0:LK, :], sel_t.astype(BF16))
        bias_s = jnp.where((picked > 0.5) & causal_s, 0.0, NEG).astype(BF16)
        os_, ls = _attend(qa_g[g], jnp.concatenate([kslc, bias_s], axis=1), vslc)
        o_slc.append(os_ / ls)

    for r in range(R):
        acc = jnp.zeros((QB, LANES), F32)
        ra, rb = r * QB, (R + r) * QB
        branches = ((o_cmp[ra:ra + QB], o_cmp[rb:rb + QB]),
                    (o_slc[0][ra:ra + QB], o_slc[1][ra:ra + QB]),
                    (o_win[ra:ra + QB], o_win[rb:rb + QB]))
        for j, (va, vb) in enumerate(branches):
            ca, cb = 3 * r + j, 3 * (R + r) + j
            gate = jnp.where(low, gates[:, ca:ca + 1], gates[:, cb:cb + 1])
            acc = acc + gate * jnp.where(low, va, vb)
        o_ref[0, rows, r * LANES:(r + 1) * LANES] = acc.astype(o_ref.dtype)


def _nsa_kernel(*refs, QB, buckets, **kw):
    needed = (pl.program_id(0) + 1) * ATT_SUB * QB

    def body(lk):
        for sub in range(ATT_SUB):
            _nsa_body(lk, sub, *refs, QB=QB, **kw)

    _for_bucket(needed, buckets, body)


def _nsa(q, gts, kc, vc, kv4):
    B, S, _ = q.shape
    QB = Q_BLOCK
    NC = kc.shape[1]
    NSEL = S // SLC_LEN
    NTOP = min(SLC_TOPN, NSEL)
    TQ = ATT_SUB * QB
    assert NSEL <= LANES and S % TQ == 0 and S >= WIN + QB
    c = np.arange(NC)[None, :] * CMP_STRIDE
    j = np.arange(LANES)[:, None] * SLC_LEN
    valid_c = np.arange(NC)[None, :] < (S - CMP_LEN) // CMP_STRIDE + 1
    ovt = ((c < j + SLC_LEN) & (c + CMP_LEN > j) & valid_c & (np.arange(LANES)[:, None] < NSEL))
    expand_t = (np.arange(S)[:, None] // SLC_LEN) == np.arange(LANES)[None, :]
    GD = NSA_KV_GROUPS * HEAD_DIM
    return pl.pallas_call(
        functools.partial(_nsa_kernel, QB=QB, NC=NC, NSEL=NSEL, NTOP=NTOP, buckets=_causal_buckets(S, 512)),
        grid=(S // TQ, B),
        in_specs=[pl.BlockSpec((1, TQ, NSA_HPG * LANES), lambda i, b: (b, i, 0)),
                  pl.BlockSpec((1, TQ, LANES), lambda i, b: (b, i, 0)),
                  pl.BlockSpec((1, NC, GD), lambda i, b: (b, 0, 0)),
                  pl.BlockSpec((1, NC, GD), lambda i, b: (b, 0, 0)),
                  pl.BlockSpec((1, S, 4 * GD), lambda i, b: (b, 0, 0)),
                  _const_spec((LANES, NC)), _const_spec((S, LANES))],
        out_specs=pl.BlockSpec((1, TQ, NSA_HPG * LANES), lambda i, b: (b, i, 0)),
        out_shape=jax.ShapeDtypeStruct((B, S, NSA_HPG * LANES), BF16),
        compiler_params=_cparams(2),
        name="nsa_attention",
    )(q, gts, kc, vc, kv4, jnp.asarray(ovt, BF16), jnp.asarray(expand_t, BF16))


def _dsa_body(LK, q_ref, k_ref, v_ref, qi_ref, ki_ref, wi_ref, o_ref, *, QB, NKEEP, CH):
    TQ = ATT_SUB * QB
    t0 = pl.program_id(0) * TQ
    low_t = lax.broadcasted_iota(jnp.int32, (TQ, LANES), 1) < HEAD_DIM
    zero = jnp.zeros((), BF16)
    causal = (lax.broadcasted_iota(jnp.int32, (LK, TQ), 0)
              <= t0 + lax.broadcasted_iota(jnp.int32, (LK, TQ), 1))

    w_t = (wi_ref[0] * (IDX_DIM ** -0.5 * IDX_HEADS ** -0.5)).T
    qi = qi_ref[0]
    ki = ki_ref[0, 0:LK, :]
    score = None
    for u in range(IDX_HEADS // 2):
        slot = qi[:, u * LANES:(u + 1) * LANES]
        pair = jnp.concatenate([jnp.where(low_t, slot, zero), jnp.where(low_t, zero, slot)], axis=0)
        sc = jnp.maximum(_dot_nt(ki, pair), 0.0)
        term = sc[:, 0:TQ] * w_t[2 * u:2 * u + 1, :] + sc[:, TQ:2 * TQ] * w_t[2 * u + 1:2 * u + 2, :]
        score = term if score is None else score + term

    bits = pltpu.bitcast(score, jnp.int32)
    key = bits ^ (lax.shift_right_arithmetic(bits, 31) & 0x7FFFFFFF)
    key = jnp.where(causal, key, KEY_NEG_INF)

    def bit_step(i, t_u):
        c_u = t_u | lax.shift_left(jnp.int32(1), 31 - i)
        cnt = _colsum(jnp.where(key >= (c_u ^ INT_MIN), 1.0, 0.0))
        return jnp.where(cnt >= NKEEP, c_u, t_u)

    thr = lax.fori_loop(0, 32, bit_step, jnp.zeros((1, TQ), jnp.int32)) ^ INT_MIN
    gt = key > thr
    eq = key == thr
    need = NKEEP - _colsum(jnp.where(gt, 1.0, 0.0))
    eq_b = jnp.where(eq, 1.0, 0.0).astype(BF16)
    tri = (lax.broadcasted_iota(jnp.int32, (CH, CH), 0)
           > lax.broadcasted_iota(jnp.int32, (CH, CH), 1)).astype(BF16)
    ranks, before = [], jnp.zeros((1, TQ), F32)
    for c in range(LK // CH):
        e = eq_b[c * CH:(c + 1) * CH]
        ranks.append(_dot(tri, e) + before)
        before = before + _colsum(e.astype(F32))
    rank = jnp.concatenate(ranks, axis=0)
    bias = jnp.where((gt | (eq & (rank < need))) & causal, 0.0, NEG).astype(BF16)
    k = k_ref[0, 0:LK, :]
    vaug = jnp.concatenate([v_ref[0, 0:LK, :], jnp.ones((LK, LANES), BF16)], axis=1)

    low = lax.broadcasted_iota(jnp.int32, (QB, LANES), 1) < HEAD_DIM
    not_low = jnp.logical_not(low)
    eye = _eye(QB)
    scale = jnp.asarray(QK_SCALE, BF16)
    HALF = DSA_HPG // 2
    for sub in range(ATT_SUB):
        rows_q = slice(sub * QB, (sub + 1) * QB)
        kaug = jnp.concatenate([k, bias[:, sub * QB:(sub + 1) * QB]], axis=1)
        for half in range(2):
            rows = []
            for r in range(half * HALF, (half + 1) * HALF):
                slot = q_ref[0, rows_q, r * LANES:(r + 1) * LANES] * scale
                rows += [_q_rows(slot, low, eye), _q_rows(slot, not_low, eye)]
            o, l = _attend(jnp.concatenate(rows, axis=0), kaug, vaug)
            o = o / l
            for i in range(HALF):
                r = half * HALF + i
                o_ref[0, rows_q, r * LANES:(r + 1) * LANES] = jnp.where(
                    low, o[2 * i * QB:(2 * i + 1) * QB], o[(2 * i + 1) * QB:(2 * i + 2) * QB]).astype(o_ref.dtype)


def _dsa_kernel(*refs, QB, buckets, **kw):
    needed = (pl.program_id(0) + 1) * ATT_SUB * QB
    _for_bucket(needed, buckets, lambda lk: _dsa_body(lk, *refs, QB=QB, **kw))


def _dsa(q, k, v, qi, ki, wi):
    B, S, _ = q.shape
    QB = Q_BLOCK
    NKEEP = min(IDX_TOPK_MAX, S // 4)
    GD = DSA_KV_HEADS * HEAD_DIM
    TQ = ATT_SUB * QB
    assert S % TQ == 0
    blk = lambda w: pl.BlockSpec((1, TQ, w), lambda i, b: (b, i, 0))
    full = lambda w: pl.BlockSpec((1, S, w), lambda i, b: (b, 0, 0))
    return pl.pallas_call(
        functools.partial(_dsa_kernel, QB=QB, NKEEP=NKEEP, CH=256, buckets=_coarse_buckets(S, TQ)),
        grid=(S // TQ, B),
        in_specs=[blk(DSA_HPG * LANES), full(GD), full(GD), blk(IDX_HEADS * IDX_DIM), full(LANES), blk(LANES)],
        out_specs=blk(DSA_HPG * LANES),
        out_shape=jax.ShapeDtypeStruct((B, S, DSA_HPG * LANES), BF16),
        compiler_params=_cparams(2),
        name="dsa_attention",
    )(q, k, v, qi, ki, wi)


def _mix_ffn_kernel(*refs, n_y, final):
    x_ref, mod_ref, gf_ref = refs[0:3]
    y_refs = refs[3:3 + n_y]
    wo_refs = refs[3 + n_y:3 + 2 * n_y]
    wg_ref, wu_ref, wd_ref = refs[3 + 2 * n_y:6 + 2 * n_y]
    rest = refs[6 + 2 * n_y:]
    o_ref = rest[-1]
    mix = None
    for y_ref, wo_ref in zip(y_refs, wo_refs):
        t = _dot(y_ref[0], wo_ref[...])
        mix = t if mix is None else mix + t
    x1 = x_ref[0] + mod_ref[0, 2:3, :] * mix
    h = (_rms(x1, gf_ref[...]) * (1.0 + mod_ref[0, 4:5, :]) + mod_ref[0, 3:4, :]).astype(BF16)
    gate = _dot(h, wg_ref[...])
    up = _dot(h, wu_ref[...])
    act = (gate * jax.nn.sigmoid(gate) * up).astype(BF16)
    x2 = x1 + mod_ref[0, 5:6, :] * _dot(act, wd_ref[...])
    if final:
        x2 = _rms(x2, rest[0][...])
    o_ref[0] = x2


def _mix_ffn(x, mod, g_ffn, ys, wos, wg, wu, wd, g_final=None, tm=512):
    B, S, D = x.shape
    FF = wg.shape[1]
    final = g_final is not None
    row_blk = lambda w: pl.BlockSpec((1, tm, w), lambda b, i: (b, i, 0))
    in_specs = ([row_blk(D), pl.BlockSpec((1, 6, D), lambda b, i: (b, 0, 0)), _const_spec((1, D))]
                + [row_blk(y.shape[2]) for y in ys]
                + [_const_spec(w.shape) for w in wos]
                + [_const_spec((D, FF)), _const_spec((D, FF)), _const_spec((FF, D))])
    args = [x, mod, g_ffn.reshape(1, D), *ys, *wos, wg, wu, wd]
    if final:
        in_specs.append(_const_spec((1, D)))
        args.append(g_final.reshape(1, D))
    return pl.pallas_call(
        functools.partial(_mix_ffn_kernel, n_y=len(ys), final=final),
        grid=(B, S // tm),
        in_specs=in_specs,
        out_specs=row_blk(D),
        out_shape=jax.ShapeDtypeStruct((B, S, D), F32),
        compiler_params=_cparams(2),
        name="mix_ffn",
    )(*args)


def _block_diag(w):
    nb, bi, bo = w.shape
    eye = jnp.eye(nb, dtype=w.dtype)
    return (w[:, :, None, :] * eye[:, None, :, None]).reshape(nb * bi, nb * bo)


def _pair_heads(n_groups, per_group):
    assert n_groups == 2
    return [g * per_group + r for r in range(per_group) for g in range(n_groups)]


def _head_cols(order):
    return np.concatenate([np.arange(h * HEAD_DIM, (h + 1) * HEAD_DIM) for h in order])


def _pad_cols(w, n):
    return jnp.pad(w, ((0, 0), (0, n - w.shape[1])))


def _compress_weights(pe, w1, w2):
    G, D, H = NSA_KV_GROUPS, HEAD_DIM, CMP_HIDDEN
    half = CMP_LEN // 2
    pe_r = jnp.broadcast_to(pe.reshape(2, half, 1, D), (2, half, G, D)).reshape(2, half * G * D)
    eye = jnp.eye(G, dtype=w1.dtype)
    w1e = (w1.reshape(2, half, 1, D, 1, H) * eye[None, None, :, None, :, None]).reshape(2, half * G * D, G * H)
    w2e = (w2[None, :, None, :] * eye[:, None, :, None]).reshape(G * H, G * D)
    return pe_r, w1e.astype(BF16), w2e.astype(BF16)


def _layer_ab(x, c, norm_mix, norm_ffn, mod_w, mod_b, w_in, conv_w, conv_b, wa, ba, wx, bx, lam,
              pe_k, w1_k, w2_k, pe_v, w1_v, w2_v, w_out, wg, wu, wd):
    B, S, D = x.shape
    mod = _adaln(c, mod_w, mod_b).reshape(B, 6, D)
    C = LRU_WIDTH
    HQ = NSA_HEADS * HEAD_DIM
    GD = NSA_KV_GROUPS * HEAD_DIM
    order = _pair_heads(NSA_KV_GROUPS, NSA_HPG)
    o_q = 2 * C
    o_kv = o_q + HQ
    o_gt = o_kv + 6 * GD
    w_q = w_in[:, o_q:o_kv][:, _head_cols(order)]
    w_all = jnp.concatenate([w_in[:, :o_q], w_q, w_in[:, o_kv:o_gt], _pad_cols(w_in[:, o_gt:], LANES)],
                            axis=1).astype(BF16)
    ag, ax, q, kcmp, vcmp, kv4, gts = _inproj(
        x, mod, norm_mix, w_all,
        [(C, F32), (C, F32), (HQ, BF16), (GD, F32), (GD, F32), (4 * GD, BF16), (LANES, F32)])
    y_a = _lru(ag, ax, conv_w, conv_b, _block_diag(wa).astype(BF16), ba, _block_diag(wx).astype(BF16), bx, lam)
    NC = S // CMP_STRIDE
    kc, vc = _compress(kcmp.reshape(B, NC, CMP_STRIDE * GD), vcmp.reshape(B, NC, CMP_STRIDE * GD),
                       *_compress_weights(pe_k, w1_k, w2_k), *_compress_weights(pe_v, w1_v, w2_v))
    y_b = _nsa(q, gts, kc, vc, kv4)
    wo_a = w_out[:C].astype(BF16)
    wo_b = w_out[C:][_head_cols(order)].astype(BF16)
    return _mix_ffn(x, mod, norm_ffn, [y_a, y_b], [wo_a, wo_b],
                    wg.astype(BF16), wu.astype(BF16), wd.astype(BF16))


def _layer_c(x, c, norm_mix, norm_ffn, mod_w, mod_b, w_in, w_out, wg, wu, wd, final_norm):
    B, S, D = x.shape
    mod = _adaln(c, mod_w, mod_b).reshape(B, 6, D)
    HQ = DSA_HEADS * HEAD_DIM
    GD = DSA_KV_HEADS * HEAD_DIM
    HI = IDX_HEADS * IDX_DIM
    order = _pair_heads(DSA_KV_HEADS, DSA_HPG)
    o_k = HQ
    o_v = o_k + GD
    o_qi = o_v + GD
    o_ki = o_qi + HI
    o_wi = o_ki + IDX_DIM
    w_ki = w_in[:, o_ki:o_wi]
    w_all = jnp.concatenate([w_in[:, :HQ][:, _head_cols(order)], w_in[:, o_k:o_qi], w_in[:, o_qi:o_ki],
                             w_ki, w_ki, _pad_cols(w_in[:, o_wi:], LANES)], axis=1).astype(BF16)
    q, k, v, qi, ki, wi = _inproj(
        x, mod, norm_mix, w_all,
        [(HQ, BF16), (GD, BF16), (GD, BF16), (HI, BF16), (LANES, BF16), (LANES, F32)])
    y = _dsa(q, k, v, qi, ki, wi)
    wo = w_out[_head_cols(order)].astype(BF16)
    return _mix_ffn(x, mod, norm_ffn, [y], [wo], wg.astype(BF16), wu.astype(BF16), wd.astype(BF16),
                    g_final=final_norm)


def kernel(x, c, l0_norm_mix, l0_norm_ffn, l0_mod_w, l0_mod_b, l0_w_in, l0_conv_w, l0_conv_b, l0_lru_wa, l0_lru_ba, l0_lru_wx, l0_lru_bx, l0_lru_lambda, l0_cmp_pe_k, l0_cmp_w1_k, l0_cmp_w2_k, l0_cmp_pe_v, l0_cmp_w1_v, l0_cmp_w2_v, l0_w_out, l0_ffn_wg, l0_ffn_wu, l0_ffn_wd, l1_norm_mix, l1_norm_ffn, l1_mod_w, l1_mod_b, l1_w_in, l1_w_out, l1_ffn_wg, l1_ffn_wu, l1_ffn_wd, final_norm):
    x = _layer_ab(x, c, l0_norm_mix, l0_norm_ffn, l0_mod_w, l0_mod_b, l0_w_in, l0_conv_w, l0_conv_b,
                  l0_lru_wa, l0_lru_ba, l0_lru_wx, l0_lru_bx, l0_lru_lambda,
                  l0_cmp_pe_k, l0_cmp_w1_k, l0_cmp_w2_k, l0_cmp_pe_v, l0_cmp_w1_v, l0_cmp_w2_v,
                  l0_w_out, l0_ffn_wg, l0_ffn_wu, l0_ffn_wd)
    return _layer_c(x, c, l1_norm_mix, l1_norm_ffn, l1_mod_w, l1_mod_b, l1_w_in, l1_w_out,
                    l1_ffn_wg, l1_ffn_wu, l1_ffn_wd, final_norm)
```

```python
import functools

import numpy as np
import jax
import jax.numpy as jnp
from jax import lax
from jax.experimental import pallas as pl
from jax.experimental.pallas import tpu as pltpu

F32 = jnp.float32
BF16 = jnp.bfloat16

EPS = 1e-6
HEAD_DIM = 64
Q_BLOCK = 128
ATT_SUB = 2
LRU_WIDTH = 512
LRU_BLOCKS = 8
CONV_WIDTH = 4
LRU_C = 8.0
NSA_HEADS = 8
NSA_KV_GROUPS = 2
NSA_HPG = NSA_HEADS // NSA_KV_GROUPS
CMP_LEN = 32
CMP_STRIDE = 16
CMP_HIDDEN = 128
SLC_LEN = 64
SLC_TOPN = 8
WIN = 512
FORCE_SCORE = 1e4
DSA_HEADS = 16
DSA_KV_HEADS = 2
DSA_HPG = DSA_HEADS // DSA_KV_HEADS
IDX_HEADS = 8
IDX_DIM = 64
IDX_TOPK_MAX = 256

LANES = 128
NEG = -1e30
VMEM_LIMIT = 56 * 1024 * 1024
QK_SCALE = HEAD_DIM ** -0.5 * float(np.log2(np.e))
FLT_MIN_BITS = 0x00800000


def _cparams(n_grid):
    return pltpu.CompilerParams(dimension_semantics=("arbitrary",) * n_grid,
                                vmem_limit_bytes=VMEM_LIMIT)


def _const_spec(shape):
    nd = len(shape)
    return pl.BlockSpec(shape, lambda *_: (0,) * nd, pipeline_mode=pl.Buffered(1))


def _dot(a, b):
    return jnp.dot(a, b, preferred_element_type=F32)


def _dot_nt(a, b):
    return lax.dot_general(a, b, (((1,), (1,)), ((), ())), preferred_element_type=F32)


def _rms(x, g):
    return x * lax.rsqrt(jnp.mean(x * x, axis=-1, keepdims=True) + EPS) * g


def _adaln_kernel(c_ref, w_ref, b_ref, o_ref):
    c = c_ref[...]
    a = (c * jax.nn.sigmoid(c)).astype(BF16)
    o_ref[...] = _dot(a, w_ref[...].astype(BF16)) + b_ref[...]


def _adaln(c, w, b):
    B, D = c.shape
    N = w.shape[1]
    tn = N // 4
    return pl.pallas_call(
        _adaln_kernel,
        grid=(N // tn,),
        in_specs=[pl.BlockSpec((B, D), lambda j: (0, 0)),
                  pl.BlockSpec((D, tn), lambda j: (0, j)),
                  pl.BlockSpec((1, tn), lambda j: (0, j))],
        out_specs=pl.BlockSpec((B, tn), lambda j: (0, j)),
        out_shape=jax.ShapeDtypeStruct((B, N), F32),
        compiler_params=_cparams(1),
        name="adaln",
    )(c, w, b.reshape(1, N))


def _inproj_kernel(x_ref, mod_ref, g_ref, w_ref, *o_refs, splits):
    x = x_ref[0]
    h = _rms(x, g_ref[...]) * (1.0 + mod_ref[0, 1:2, :]) + mod_ref[0, 0:1, :]
    acc = _dot(h.astype(BF16), w_ref[...])
    for o_ref, (c0, wd) in zip(o_refs, splits):
        o_ref[0] = acc[:, c0:c0 + wd].astype(o_ref.dtype)


def _inproj(x, mod, g, w, outs, tm=512):
    B, S, D = x.shape
    N = w.shape[1]
    splits, c0 = [], 0
    for wd, _ in outs:
        splits.append((c0, wd))
        c0 += wd
    assert c0 == N
    return pl.pallas_call(
        functools.partial(_inproj_kernel, splits=tuple(splits)),
        grid=(B, S // tm),
        in_specs=[pl.BlockSpec((1, tm, D), lambda b, i: (b, i, 0)),
                  pl.BlockSpec((1, 6, D), lambda b, i: (b, 0, 0)),
                  _const_spec((1, D)),
                  _const_spec((D, N))],
        out_specs=[pl.BlockSpec((1, tm, wd), lambda b, i: (b, i, 0)) for wd, _ in outs],
        out_shape=[jax.ShapeDtypeStruct((B, S, wd), dt) for wd, dt in outs],
        compiler_params=_cparams(2),
        name="inproj",
    )(x, mod, g.reshape(1, D), w)


def _lru_kernel(ag_ref, ax_ref, cw_ref, cb_ref, wa_ref, ba_ref, wx_ref, bx_ref, lam_ref, o_ref,
                xpad, a_s, u_s, *, S, C, TC):
    PAD = 8
    xpad[0:PAD, :] = jnp.zeros((PAD, C), F32)
    xpad[PAD:PAD + S, :] = ax_ref[0]
    z = -lam_ref[...]
    sp = jnp.maximum(z, 0.0) + jnp.log(1.0 + jnp.exp(-jnp.abs(z)))
    for ci in range(S // TC):
        r0 = ci * TC
        xc = cb_ref[...]
        for k in range(CONV_WIDTH):
            off = PAD - (CONV_WIDTH - 1) + k + r0
            xc = xc + xpad[off:off + TC, :] * cw_ref[k:k + 1, :]
        xb = xc.astype(BF16)
        r = jax.nn.sigmoid(_dot(xb, wa_ref[...]) + ba_ref[...])
        i = jax.nn.sigmoid(_dot(xb, wx_ref[...]) + bx_ref[...])
        log_a = -LRU_C * r * sp
        a = jnp.exp(log_a)
        mult = jnp.sqrt(1.0 - jnp.exp(2.0 * log_a))
        if ci == 0:
            row = lax.broadcasted_iota(jnp.int32, (TC, C), 0)
            mult = jnp.where(row == 0, 1.0, mult)
        a_s[r0:r0 + TC, :] = a
        u_s[r0:r0 + TC, :] = mult * (i * xc)

    row8 = lax.broadcasted_iota(jnp.int32, (8, C), 0)

    def tile(ti, h):
        t0 = pl.multiple_of(ti * 8, 8)
        A = a_s[pl.ds(t0, 8), :]
        U = u_s[pl.ds(t0, 8), :]
        for d in (1, 2, 4):
            As = pltpu.roll(A, d, 0)
            Us = pltpu.roll(U, d, 0)
            ok = row8 >= d
            U = jnp.where(ok, A * Us + U, U)
            A = jnp.where(ok, A * As, A)
        H = U + A * h
        u_s[pl.ds(t0, 8), :] = H
        return H[7:8, :]

    lax.fori_loop(0, S // 8, tile, jnp.zeros((1, C), F32))

    for ci in range(S // TC):
        r0 = ci * TC
        o_ref[0, r0:r0 + TC, :] = (u_s[r0:r0 + TC, :]
                                   * jax.nn.gelu(ag_ref[0, r0:r0 + TC, :])).astype(o_ref.dtype)


def _lru(ag, ax, cw, cb, wa_bd, ba, wx_bd, bx, lam):
    B, S, C = ag.shape
    TC = 256
    row = lambda v: v.reshape(1, C)
    return pl.pallas_call(
        functools.partial(_lru_kernel, S=S, C=C, TC=TC),
        grid=(B,),
        in_specs=[pl.BlockSpec((1, S, C), lambda b: (b, 0, 0)),
                  pl.BlockSpec((1, S, C), lambda b: (b, 0, 0)),
                  _const_spec((CONV_WIDTH, C)), _const_spec((1, C)),
                  _const_spec((C, C)), _const_spec((1, C)),
                  _const_spec((C, C)), _const_spec((1, C)), _const_spec((1, C))],
        out_specs=pl.BlockSpec((1, S, C), lambda b: (b, 0, 0)),
        out_shape=jax.ShapeDtypeStruct((B, S, C), BF16),
        scratch_shapes=[pltpu.VMEM((S + 8, C), F32), pltpu.VMEM((S, C), F32), pltpu.VMEM((S, C), F32)],
        compiler_params=_cparams(1),
        name="rglru",
    )(ag, ax, cw, row(cb), wa_bd, row(ba), wx_bd, row(bx), row(lam))


def _compress_kernel(xk_ref, xv_ref, pek_ref, w1k_ref, w2k_ref, pev_ref, w1v_ref, w2v_ref,
                     ok_ref, ov_ref):
    def one(x_ref, pe_ref, w1_ref, w2_ref, o_ref):
        x = x_ref[0]
        p0 = _dot((x + pe_ref[0:1, :]).astype(BF16), w1_ref[0])
        p1 = _dot((x + pe_ref[1:2, :]).astype(BF16), w1_ref[1])
        nc = x.shape[0]
        hid = p0 + pltpu.roll(p1, nc - 1, 0)
        o_ref[0] = _dot(jax.nn.gelu(hid).astype(BF16), w2_ref[...]).astype(o_ref.dtype)

    one(xk_ref, pek_ref, w1k_ref, w2k_ref, ok_ref)
    one(xv_ref, pev_ref, w1v_ref, w2v_ref, ov_ref)


def _compress(xk, xv, pek, w1k, w2k, pev, w1v, w2v):
    B, NC, W = xk.shape
    GH = NSA_KV_GROUPS * CMP_HIDDEN
    GD = NSA_KV_GROUPS * HEAD_DIM
    xspec = pl.BlockSpec((1, NC, W), lambda b: (b, 0, 0))
    ospec = pl.BlockSpec((1, NC, GD), lambda b: (b, 0, 0))
    return pl.pallas_call(
        _compress_kernel,
        grid=(B,),
        in_specs=[xspec, xspec,
                  _const_spec((2, W)), _const_spec((2, W, GH)), _const_spec((GH, GD)),
                  _const_spec((2, W)), _const_spec((2, W, GH)), _const_spec((GH, GD))],
        out_specs=[ospec, ospec],
        out_shape=[jax.ShapeDtypeStruct((B, NC, GD), BF16)] * 2,
        compiler_params=_cparams(1),
        name="nsa_compress",
    )(xk, xv, pek, w1k, w2k, pev, w1v, w2v)


def _causal_buckets(S, step):
    step = min(S, step)
    assert S % step == 0
    return tuple(range(step, S + 1, step))


def _coarse_buckets(S, align):
    up = lambda n: -(-n // align) * align
    return tuple(sorted({up(S // 4), up(5 * S // 8), S}))


def _for_bucket(needed, buckets, body):
    lo = 0
    for lk in buckets:
        pl.when((needed > lo) & (needed <= lk))(functools.partial(body, lk))
        lo = lk


def _eye(QB):
    return (lax.broadcasted_iota(jnp.int32, (QB, LANES), 0)
            == lax.broadcasted_iota(jnp.int32, (QB, LANES), 1)).astype(BF16)


def _q_rows(slot, keep, eye):
    return jnp.concatenate([jnp.where(keep, slot, jnp.zeros((), BF16)), eye], axis=1)


def _colsum(x):
    n, q = x.shape
    part = jnp.sum(x.reshape(n // 64, 64, q), axis=0)
    return jnp.sum(part, axis=0, keepdims=True)


def _colreduce(x, op):
    n, q = x.shape
    return op(op(x.reshape(n // 64, 64, q), axis=0), axis=0, keepdims=True)


def _f32_to_rank(x):
    bits = pltpu.bitcast(x, jnp.int32)
    key = bits ^ (lax.shift_right_arithmetic(bits, 31) & 0x7FFFFFFF)
    return jnp.where(key >= FLT_MIN_BITS, key - (FLT_MIN_BITS - 1), jnp.where(key < -FLT_MIN_BITS, key + FLT_MIN_BITS, 0))


def _rank_to_f32(r):
    key = jnp.where(r > 0, r + (FLT_MIN_BITS - 1), jnp.where(r < 0, r - FLT_MIN_BITS, 0))
    return pltpu.bitcast(key ^ (lax.shift_right_arithmetic(key, 31) & 0x7FFFFFFF), F32)


def _kth_largest(x, x_valid_min, n_valid, k):
    few = n_valid <= k
    lo0 = _f32_to_rank(_colreduce(x_valid_min, jnp.min))
    hi0 = _f32_to_rank(_colreduce(x, jnp.max)) + 1
    done0 = jnp.where(few | (lo0 + 1 == hi0), 1, 0).astype(jnp.int32)
    FIRST_STEPS, CHECK_EVERY, VALUE_STEPS, MAX_STEPS = 20, 4, 24, 64

    def step(u, state):
        it, lo, hi, lof, hif, done = state
        mid_f = 0.5 * lof + 0.5 * hif
        mid_v = _f32_to_rank(mid_f)
        mid_k = lax.shift_right_arithmetic(lo, 1) + lax.shift_right_arithmetic(hi, 1) + (lo & hi & 1)
        use_v = (mid_v > lo) & (mid_v < hi) & (it < VALUE_STEPS)
        near0 = jnp.where((lo < 0) & (hi > 0), 0, jnp.where(lo == 0, 1, -1))
        zero_step = ((lo < 0) & (hi > 0)) | (lo == 0) | (hi == 0)
        mid_k = jnp.where(zero_step, near0, mid_k)
        use_v = use_v & jnp.logical_not(zero_step)
        mid = jnp.where(use_v, mid_v, mid_k)
        midf = jnp.where(use_v, mid_f, _rank_to_f32(mid_k))
        cnt = _colsum(jnp.where(x >= midf, 1.0, 0.0))
        up = (done == 0) & (cnt >= k)
        down = (done == 0) & (cnt < k)
        lo, lof = jnp.where(up, mid, lo), jnp.where(up, midf, lof)
        hi, hif = jnp.where(down, mid, hi), jnp.where(down, midf, hif)
        done = jnp.where((cnt == k) | (lo + 1 == hi), 1, done)
        return it + 1, lo, hi, lof, hif, done

    def body(carry):
        state = lax.fori_loop(0, CHECK_EVERY, step, carry[0:6])
        return (*state, jnp.min(state[5].astype(F32)))

    def cond(carry):
        return (carry[0] < MAX_STEPS) & (carry[6] < 0.5)

    state = lax.fori_loop(0, FIRST_STEPS, step, (jnp.int32(0), lo0, hi0, _rank_to_f32(lo0), _rank_to_f32(hi0), done0))
    lof = lax.while_loop(cond, body, (*state, jnp.min(state[5].astype(F32))))[3]
    return jnp.where(few, -jnp.inf, lof)


def _attend(qa, kaug, vaug):
    s = _dot_nt(qa, kaug)
    m = jnp.max(s, axis=-1, keepdims=True)
    oa = _dot(jnp.exp2(s - m).astype(BF16), vaug)
    return oa[:, 0:LANES], oa[:, LANES:2 * LANES]


def _nsa_body(LK, sub, q_ref, gt_ref, kc_ref, vc_ref, kv_ref, ovt_ref, expt_ref, o_ref, *, QB, NC, NSEL, NTOP):
    t0 = (pl.program_id(0) * ATT_SUB + sub) * QB
    rows = slice(sub * QB, (sub + 1) * QB)
    R = NSA_HPG
    G = NSA_KV_GROUPS
    low = lax.broadcasted_iota(jnp.int32, (QB, LANES), 1) < HEAD_DIM
    keep_g = (low, jnp.logical_not(low))
    eye = _eye(QB)
    q = (q_ref[0, rows, :].astype(F32) * QK_SCALE).astype(BF16)
    gates = jax.nn.sigmoid(gt_ref[0, rows, :])
    qa_g = [jnp.concatenate([_q_rows(q[:, r * LANES:(r + 1) * LANES], keep_g[g], eye) for r in range(R)], axis=0)
            for g in range(G)]
    qa = jnp.concatenate(qa_g, axis=0)

    def tq_t(n):
        return t0 + lax.broadcasted_iota(jnp.int32, (n, QB), 1)

    def vaug(v):
        return jnp.concatenate([v, jnp.ones(v.shape, BF16)], axis=1)

    cend = lax.broadcasted_iota(jnp.int32, (NC, QB), 0) * CMP_STRIDE + (CMP_LEN - 1)
    bias_c = jnp.where(cend <= tq_t(NC), 0.0, NEG).astype(BF16)
    s = _dot_nt(qa, jnp.concatenate([kc_ref[0], bias_c], axis=1))
    m = jnp.max(s, axis=-1, keepdims=True)
    p = jnp.exp2(s - m)
    inv = jnp.where(m > 0.5 * NEG, 1.0 / jnp.maximum(jnp.sum(p, axis=-1, keepdims=True), 1e-30), 0.0)
    o_cmp = _dot(p.astype(BF16), vc_ref[0]) * inv
    pn = p * inv

    start = pl.multiple_of(jnp.maximum(t0 - WIN, 0), QB)
    kpos_w = start + lax.broadcasted_iota(jnp.int32, (WIN + QB, QB), 0)
    bias_w = jnp.where((kpos_w <= tq_t(WIN + QB)) & (kpos_w > tq_t(WIN + QB) - WIN), 0.0, NEG).astype(BF16)
    kwin = kv_ref[0, pl.ds(start, WIN + QB), 2 * LANES:3 * LANES]
    vwin = kv_ref[0, pl.ds(start, WIN + QB), 3 * LANES:4 * LANES]
    ow, lw = _attend(qa, jnp.concatenate([kwin, bias_w], axis=1), vaug(vwin))
    o_win = ow / lw

    jrow = lax.broadcasted_iota(jnp.int32, (LANES, QB), 0)
    cur = lax.shift_right_logical(tq_t(LANES), 6)
    forced = (jrow == 0) | (jrow == cur) | (jrow == cur - 1)
    causal_s = lax.broadcasted_iota(jnp.int32, (LK, QB), 0) <= tq_t(LK)
    kslc = kv_ref[0, 0:LK, 0:LANES]
    vslc = vaug(kv_ref[0, 0:LK, LANES:2 * LANES])
    o_slc = []
    for g in range(G):
        base = g * R * QB
        psum = pn[base:base + QB]
        for r in range(1, R):
            psum = psum + pn[base + r * QB:base + (r + 1) * QB]
        hi = psum.astype(BF16)
        lo = (psum - hi.astype(F32)).astype(BF16)
        imp = _dot_nt(ovt_ref[...], hi) + _dot_nt(ovt_ref[...], lo)
        imp = jnp.where(forced, FORCE_SCORE, imp)
        imp = jnp.where(jrow <= cur, imp, -jnp.inf)
        imp = imp[0:NSEL]
        jr = jrow[0:NSEL]
        cnt = jnp.zeros((NSEL, QB), F32)
        for j2 in range(NSEL):
            rowv = imp[j2:j2 + 1, :]
            cnt = cnt + jnp.where(jr > j2, jnp.where(rowv >= imp, 1.0, 0.0), jnp.where(rowv > imp, 1.0, 0.0))
        sel_t = jnp.where(cnt < NTOP, 1.0, 0.0)
        if NSEL < LANES:
            sel_t = jnp.concatenate([sel_t, jnp.zeros((LANES - NSEL, QB), F32)], axis=0)
        picked = _dot(expt_ref[0:LK, :], sel_t.astype(BF16))
        bias_s = jnp.where((picked > 0.5) & causal_s, 0.0, NEG).astype(BF16)
        os_, ls = _attend(qa_g[g], jnp.concatenate([kslc, bias_s], axis=1), vslc)
        o_slc.append(os_ / ls)

    for r in range(R):
        acc = jnp.zeros((QB, LANES), F32)
        ra, rb = r * QB, (R + r) * QB
        branches = ((o_cmp[ra:ra + QB], o_cmp[rb:rb + QB]),
                    (o_slc[0][ra:ra + QB], o_slc[1][ra:ra + QB]),
                    (o_win[ra:ra + QB], o_win[rb:rb + QB]))
        for j, (va, vb) in enumerate(branches):
            ca, cb = 3 * r + j, 3 * (R + r) + j
            gate = jnp.where(low, gates[:, ca:ca + 1], gates[:, cb:cb + 1])
            acc = acc + gate * jnp.where(low, va, vb)
        o_ref[0, rows, r * LANES:(r + 1) * LANES] = acc.astype(o_ref.dtype)


def _nsa_kernel(*refs, QB, buckets, **kw):
    needed = (pl.program_id(0) + 1) * ATT_SUB * QB

    def body(lk):
        for sub in range(ATT_SUB):
            _nsa_body(lk, sub, *refs, QB=QB, **kw)

    _for_bucket(needed, buckets, body)


def _nsa(q, gts, kc, vc, kv4):
    B, S, _ = q.shape
    QB = Q_BLOCK
    NC = kc.shape[1]
    NSEL = S // SLC_LEN
    NTOP = min(SLC_TOPN, NSEL)
    TQ = ATT_SUB * QB
    assert NSEL <= LANES and S % TQ == 0 and S >= WIN + QB
    c = np.arange(NC)[None, :] * CMP_STRIDE
    j = np.arange(LANES)[:, None] * SLC_LEN
    valid_c = np.arange(NC)[None, :] < (S - CMP_LEN) // CMP_STRIDE + 1
    ovt = ((c < j + SLC_LEN) & (c + CMP_LEN > j) & valid_c & (np.arange(LANES)[:, None] < NSEL))
    expand_t = (np.arange(S)[:, None] // SLC_LEN) == np.arange(LANES)[None, :]
    GD = NSA_KV_GROUPS * HEAD_DIM
    return pl.pallas_call(
        functools.partial(_nsa_kernel, QB=QB, NC=NC, NSEL=NSEL, NTOP=NTOP, buckets=_causal_buckets(S, 512)),
        grid=(S // TQ, B),
        in_specs=[pl.BlockSpec((1, TQ, NSA_HPG * LANES), lambda i, b: (b, i, 0)),
                  pl.BlockSpec((1, TQ, LANES), lambda i, b: (b, i, 0)),
                  pl.BlockSpec((1, NC, GD), lambda i, b: (b, 0, 0)),
                  pl.BlockSpec((1, NC, GD), lambda i, b: (b, 0, 0)),
                  pl.BlockSpec((1, S, 4 * GD), lambda i, b: (b, 0, 0)),
                  _const_spec((LANES, NC)), _const_spec((S, LANES))],
        out_specs=pl.BlockSpec((1, TQ, NSA_HPG * LANES), lambda i, b: (b, i, 0)),
        out_shape=jax.ShapeDtypeStruct((B, S, NSA_HPG * LANES), BF16),
        compiler_params=_cparams(2),
        name="nsa_attention",
    )(q, gts, kc, vc, kv4, jnp.asarray(ovt, BF16), jnp.asarray(expand_t, BF16))


def _dsa_body(LK, q_ref, k_ref, v_ref, qi_ref, ki_ref, wi_ref, o_ref, *, QB, NKEEP, CH):
    TQ = ATT_SUB * QB
    t0 = pl.program_id(0) * TQ
    low_t = lax.broadcasted_iota(jnp.int32, (TQ, LANES), 1) < HEAD_DIM
    zero = jnp.zeros((), BF16)
    causal = (lax.broadcasted_iota(jnp.int32, (LK, TQ), 0)
              <= t0 + lax.broadcasted_iota(jnp.int32, (LK, TQ), 1))

    w_t = (wi_ref[0] * IDX_DIM ** -0.5).T
    qi = qi_ref[0]
    ki = ki_ref[0, 0:LK, :]
    score = None
    for u in range(IDX_HEADS // 2):
        slot = qi[:, u * LANES:(u + 1) * LANES]
        pair = jnp.concatenate([jnp.where(low_t, slot, zero), jnp.where(low_t, zero, slot)], axis=0)
        sc = jnp.maximum(_dot_nt(ki, pair), 0.0)
        term = sc[:, 0:TQ] * w_t[2 * u:2 * u + 1, :] + sc[:, TQ:2 * TQ] * w_t[2 * u + 1:2 * u + 2, :]
        score = term if score is None else score + term

    score = score * IDX_HEADS ** -0.5
    masked = jnp.where(causal, score, -jnp.inf)
    n_valid = t0 + lax.broadcasted_iota(jnp.int32, (1, TQ), 1) + 1
    thr = _kth_largest(masked, jnp.where(causal, score, jnp.inf), n_valid, NKEEP)
    gt = masked > thr
    eq = masked == thr
    need = NKEEP - _colsum(jnp.where(gt, 1.0, 0.0))
    eq_b = jnp.where(eq, 1.0, 0.0).astype(BF16)
    tri = (lax.broadcasted_iota(jnp.int32, (CH, CH), 0)
           > lax.broadcasted_iota(jnp.int32, (CH, CH), 1)).astype(BF16)
    ranks, before = [], jnp.zeros((1, TQ), F32)
    for c in range(LK // CH):
        e = eq_b[c * CH:(c + 1) * CH]
        ranks.append(_dot(tri, e) + before)
        before = before + _colsum(e.astype(F32))
    rank = jnp.concatenate(ranks, axis=0)
    bias = jnp.where((gt | (eq & (rank < need))) & causal, 0.0, NEG).astype(BF16)
    k = k_ref[0, 0:LK, :]
    vaug = jnp.concatenate([v_ref[0, 0:LK, :], jnp.ones((LK, LANES), BF16)], axis=1)

    low = lax.broadcasted_iota(jnp.int32, (QB, LANES), 1) < HEAD_DIM
    not_low = jnp.logical_not(low)
    eye = _eye(QB)
    HALF = DSA_HPG // 2
    for sub in range(ATT_SUB):
        rows_q = slice(sub * QB, (sub + 1) * QB)
        kaug = jnp.concatenate([k, bias[:, sub * QB:(sub + 1) * QB]], axis=1)
        for half in range(2):
            rows = []
            for r in range(half * HALF, (half + 1) * HALF):
                slot = (q_ref[0, rows_q, r * LANES:(r + 1) * LANES].astype(F32) * QK_SCALE).astype(BF16)
                rows += [_q_rows(slot, low, eye), _q_rows(slot, not_low, eye)]
            o, l = _attend(jnp.concatenate(rows, axis=0), kaug, vaug)
            o = o / l
            for i in range(HALF):
                r = half * HALF + i
                o_ref[0, rows_q, r * LANES:(r + 1) * LANES] = jnp.where(
                    low, o[2 * i * QB:(2 * i + 1) * QB], o[(2 * i + 1) * QB:(2 * i + 2) * QB]).astype(o_ref.dtype)


def _dsa_kernel(*refs, QB, buckets, **kw):
    needed = (pl.program_id(0) + 1) * ATT_SUB * QB
    _for_bucket(needed, buckets, lambda lk: _dsa_body(lk, *refs, QB=QB, **kw))


def _dsa(q, k, v, qi, ki, wi):
    B, S, _ = q.shape
    QB = Q_BLOCK
    NKEEP = min(IDX_TOPK_MAX, S // 4)
    GD = DSA_KV_HEADS * HEAD_DIM
    TQ = ATT_SUB * QB
    assert S % TQ == 0
    blk = lambda w: pl.BlockSpec((1, TQ, w), lambda i, b: (b, i, 0))
    full = lambda w: pl.BlockSpec((1, S, w), lambda i, b: (b, 0, 0))
    return pl.pallas_call(
        functools.partial(_dsa_kernel, QB=QB, NKEEP=NKEEP, CH=256, buckets=_coarse_buckets(S, TQ)),
        grid=(S // TQ, B),
        in_specs=[blk(DSA_HPG * LANES), full(GD), full(GD), blk(IDX_HEADS * IDX_DIM), full(LANES), blk(LANES)],
        out_specs=blk(DSA_HPG * LANES),
        out_shape=jax.ShapeDtypeStruct((B, S, DSA_HPG * LANES), BF16),
        compiler_params=_cparams(2),
        name="dsa_attention",
    )(q, k, v, qi, ki, wi)


def _mix_ffn_kernel(*refs, n_y, final):
    x_ref, mod_ref, gf_ref = refs[0:3]
    y_refs = refs[3:3 + n_y]
    wo_refs = refs[3 + n_y:3 + 2 * n_y]
    wg_ref, wu_ref, wd_ref = refs[3 + 2 * n_y:6 + 2 * n_y]
    rest = refs[6 + 2 * n_y:]
    o_ref = rest[-1]
    mix = None
    for y_ref, wo_ref in zip(y_refs, wo_refs):
        t = _dot(y_ref[0], wo_ref[...])
        mix = t if mix is None else mix + t
    x1 = x_ref[0] + mod_ref[0, 2:3, :] * mix
    h = (_rms(x1, gf_ref[...]) * (1.0 + mod_ref[0, 4:5, :]) + mod_ref[0, 3:4, :]).astype(BF16)
    gate = _dot(h, wg_ref[...])
    up = _dot(h, wu_ref[...])
    act = (gate * jax.nn.sigmoid(gate) * up).astype(BF16)
    x2 = x1 + mod_ref[0, 5:6, :] * _dot(act, wd_ref[...])
    if final:
        x2 = _rms(x2, rest[0][...])
    o_ref[0] = x2


def _mix_ffn(x, mod, g_ffn, ys, wos, wg, wu, wd, g_final=None, tm=512):
    B, S, D = x.shape
    FF = wg.shape[1]
    final = g_final is not None
    row_blk = lambda w: pl.BlockSpec((1, tm, w), lambda b, i: (b, i, 0))
    in_specs = ([row_blk(D), pl.BlockSpec((1, 6, D), lambda b, i: (b, 0, 0)), _const_spec((1, D))]
                + [row_blk(y.shape[2]) for y in ys]
                + [_const_spec(w.shape) for w in wos]
                + [_const_spec((D, FF)), _const_spec((D, FF)), _const_spec((FF, D))])
    args = [x, mod, g_ffn.reshape(1, D), *ys, *wos, wg, wu, wd]
    if final:
        in_specs.append(_const_spec((1, D)))
        args.append(g_final.reshape(1, D))
    return pl.pallas_call(
        functools.partial(_mix_ffn_kernel, n_y=len(ys), final=final),
        grid=(B, S // tm),
        in_specs=in_specs,
        out_specs=row_blk(D),
        out_shape=jax.ShapeDtypeStruct((B, S, D), F32),
        compiler_params=_cparams(2),
        name="mix_ffn",
    )(*args)


def _block_diag(w):
    nb, bi, bo = w.shape
    eye = jnp.eye(nb, dtype=w.dtype)
    return (w[:, :, None, :] * eye[:, None, :, None]).reshape(nb * bi, nb * bo)


def _pair_heads(n_groups, per_group):
    assert n_groups == 2
    return [g * per_group + r for r in range(per_group) for g in range(n_groups)]


def _head_cols(order):
    return np.concatenate([np.arange(h * HEAD_DIM, (h + 1) * HEAD_DIM) for h in order])


def _pad_cols(w, n):
    return jnp.pad(w, ((0, 0), (0, n - w.shape[1])))


def _compress_weights(pe, w1, w2):
    G, D, H = NSA_KV_GROUPS, HEAD_DIM, CMP_HIDDEN
    half = CMP_LEN // 2
    pe_r = jnp.broadcast_to(pe.reshape(2, half, 1, D), (2, half, G, D)).reshape(2, half * G * D)
    eye = jnp.eye(G, dtype=w1.dtype)
    w1e = (w1.reshape(2, half, 1, D, 1, H) * eye[None, None, :, None, :, None]).reshape(2, half * G * D, G * H)
    w2e = (w2[None, :, None, :] * eye[:, None, :, None]).reshape(G * H, G * D)
    return pe_r, w1e.astype(BF16), w2e.astype(BF16)


def _layer_ab(x, c, norm_mix, norm_ffn, mod_w, mod_b, w_in, conv_w, conv_b, wa, ba, wx, bx, lam,
              pe_k, w1_k, w2_k, pe_v, w1_v, w2_v, w_out, wg, wu, wd):
    B, S, D = x.shape
    mod = _adaln(c, mod_w, mod_b).reshape(B, 6, D)
    C = LRU_WIDTH
    HQ = NSA_HEADS * HEAD_DIM
    GD = NSA_KV_GROUPS * HEAD_DIM
    order = _pair_heads(NSA_KV_GROUPS, NSA_HPG)
    o_q = 2 * C
    o_kv = o_q + HQ
    o_gt = o_kv + 6 * GD
    w_q = w_in[:, o_q:o_kv][:, _head_cols(order)]
    w_all = jnp.concatenate([w_in[:, :o_q], w_q, w_in[:, o_kv:o_gt], _pad_cols(w_in[:, o_gt:], LANES)],
                            axis=1).astype(BF16)
    ag, ax, q, kcmp, vcmp, kv4, gts = _inproj(
        x, mod, norm_mix, w_all,
        [(C, F32), (C, F32), (HQ, BF16), (GD, F32), (GD, F32), (4 * GD, BF16), (LANES, F32)])
    y_a = _lru(ag, ax, conv_w, conv_b, _block_diag(wa).astype(BF16), ba, _block_diag(wx).astype(BF16), bx, lam)
    NC = S // CMP_STRIDE
    kc, vc = _compress(kcmp.reshape(B, NC, CMP_STRIDE * GD), vcmp.reshape(B, NC, CMP_STRIDE * GD),
                       *_compress_weights(pe_k, w1_k, w2_k), *_compress_weights(pe_v, w1_v, w2_v))
    y_b = _nsa(q, gts, kc, vc, kv4)
    wo_a = w_out[:C].astype(BF16)
    wo_b = w_out[C:][_head_cols(order)].astype(BF16)
    return _mix_ffn(x, mod, norm_ffn, [y_a, y_b], [wo_a, wo_b],
                    wg.astype(BF16), wu.astype(BF16), wd.astype(BF16))


def _layer_c(x, c, norm_mix, norm_ffn, mod_w, mod_b, w_in, w_out, wg, wu, wd, final_norm):
    B, S, D = x.shape
    mod = _adaln(c, mod_w, mod_b).reshape(B, 6, D)
    HQ = DSA_HEADS * HEAD_DIM
    GD = DSA_KV_HEADS * HEAD_DIM
    HI = IDX_HEADS * IDX_DIM
    order = _pair_heads(DSA_KV_HEADS, DSA_HPG)
    o_k = HQ
    o_v = o_k + GD
    o_qi = o_v + GD
    o_ki = o_qi + HI
    o_wi = o_ki + IDX_DIM
    w_ki = w_in[:, o_ki:o_wi]
    w_all = jnp.concatenate([w_in[:, :HQ][:, _head_cols(order)], w_in[:, o_k:o_qi], w_in[:, o_qi:o_ki],
                             w_ki, w_ki, _pad_cols(w_in[:, o_wi:], LANES)], axis=1).astype(BF16)
    q, k, v, qi, ki, wi = _inproj(
        x, mod, norm_mix, w_all,
        [(HQ, BF16), (GD, BF16), (GD, BF16), (HI, BF16), (LANES, BF16), (LANES, F32)])
    y = _dsa(q, k, v, qi, ki, wi)
    wo = w_out[_head_cols(order)].astype(BF16)
    return _mix_ffn(x, mod, norm_ffn, [y], [wo], wg.astype(BF16), wu.astype(BF16), wd.astype(BF16),
                    g_final=final_norm)


def kernel(x, c, l0_norm_mix, l0_norm_ffn, l0_mod_w, l0_mod_b, l0_w_in, l0_conv_w, l0_conv_b, l0_lru_wa, l0_lru_ba, l0_lru_wx, l0_lru_bx, l0_lru_lambda, l0_cmp_pe_k, l0_cmp_w1_k, l0_cmp_w2_k, l0_cmp_pe_v, l0_cmp_w1_v, l0_cmp_w2_v, l0_w_out, l0_ffn_wg, l0_ffn_wu, l0_ffn_wd, l1_norm_mix, l1_norm_ffn, l1_mod_w, l1_mod_b, l1_w_in, l1_w_out, l1_ffn_wg, l1_ffn_wu, l1_ffn_wd, final_norm):
    x = _layer_ab(x, c, l0_norm_mix, l0_norm_ffn, l0_mod_w, l0_mod_b, l0_w_in, l0_conv_w, l0_conv_b,
                  l0_lru_wa, l0_lru_ba, l0_lru_wx, l0_lru_bx, l0_lru_lambda,
                  l0_cmp_pe_k, l0_cmp_w1_k, l0_cmp_w2_k, l0_cmp_pe_v, l0_cmp_w1_v, l0_cmp_w2_v,
                  l0_w_out, l0_ffn_wg, l0_ffn_wu, l0_ffn_wd)
    return _layer_c(x, c, l1_norm_mix, l1_norm_ffn, l1_mod_w, l1_mod_b, l1_w_in, l1_w_out,
                    l1_ffn_wg, l1_ffn_wu, l1_ffn_wd, final_norm)
```

```python
import functools

import numpy as np
import jax
import jax.numpy as jnp
from jax import lax
from jax.experimental import pallas as pl
from jax.experimental.pallas import tpu as pltpu

F32 = jnp.float32
BF16 = jnp.bfloat16

EPS = 1e-6
HEAD_DIM = 64
Q_BLOCK = 128
ATT_SUB = 2
LRU_WIDTH = 512
LRU_BLOCKS = 8
CONV_WIDTH = 4
LRU_C = 8.0
NSA_HEADS = 8
NSA_KV_GROUPS = 2
NSA_HPG = NSA_HEADS // NSA_KV_GROUPS
CMP_LEN = 32
CMP_STRIDE = 16
CMP_HIDDEN = 128
SLC_LEN = 64
SLC_TOPN = 8
WIN = 512
FORCE_SCORE = 1e4
DSA_HEADS = 16
DSA_KV_HEADS = 2
DSA_HPG = DSA_HEADS // DSA_KV_HEADS
IDX_HEADS = 8
IDX_DIM = 64
IDX_TOPK_MAX = 256

LANES = 128
NEG = -1e30
VMEM_LIMIT = 56 * 1024 * 1024
QK_SCALE = HEAD_DIM ** -0.5 * float(np.log2(np.e))
FLT_MIN_BITS = 0x00800000
DSA_KEYS_PER_PROGRAM = 3584


def _cparams(n_grid):
    return pltpu.CompilerParams(dimension_semantics=("arbitrary",) * n_grid,
                                vmem_limit_bytes=VMEM_LIMIT)


def _const_spec(shape):
    nd = len(shape)
    return pl.BlockSpec(shape, lambda *_: (0,) * nd, pipeline_mode=pl.Buffered(1))


def _dot(a, b):
    return jnp.dot(a, b, preferred_element_type=F32)


def _dot_nt(a, b):
    return lax.dot_general(a, b, (((1,), (1,)), ((), ())), preferred_element_type=F32)


def _rms(x, g):
    return x * lax.rsqrt(jnp.mean(x * x, axis=-1, keepdims=True) + EPS) * g


def _adaln_kernel(c_ref, w_ref, b_ref, o_ref):
    c = c_ref[...]
    a = (c * jax.nn.sigmoid(c)).astype(BF16)
    o_ref[...] = _dot(a, w_ref[...].astype(BF16)) + b_ref[...]


def _adaln(c, w, b):
    B, D = c.shape
    N = w.shape[1]
    tn = N // 4
    return pl.pallas_call(
        _adaln_kernel,
        grid=(N // tn,),
        in_specs=[pl.BlockSpec((B, D), lambda j: (0, 0)),
                  pl.BlockSpec((D, tn), lambda j: (0, j)),
                  pl.BlockSpec((1, tn), lambda j: (0, j))],
        out_specs=pl.BlockSpec((B, tn), lambda j: (0, j)),
        out_shape=jax.ShapeDtypeStruct((B, N), F32),
        compiler_params=_cparams(1),
        name="adaln",
    )(c, w, b.reshape(1, N))


def _inproj_kernel(x_ref, mod_ref, g_ref, w_ref, *o_refs, splits):
    x = x_ref[0]
    h = _rms(x, g_ref[...]) * (1.0 + mod_ref[0, 1:2, :]) + mod_ref[0, 0:1, :]
    acc = _dot(h.astype(BF16), w_ref[...])
    for o_ref, (c0, wd) in zip(o_refs, splits):
        o_ref[0] = acc[:, c0:c0 + wd].astype(o_ref.dtype)


def _inproj(x, mod, g, w, outs, tm=512):
    B, S, D = x.shape
    N = w.shape[1]
    splits, c0 = [], 0
    for wd, _ in outs:
        splits.append((c0, wd))
        c0 += wd
    assert c0 == N
    return pl.pallas_call(
        functools.partial(_inproj_kernel, splits=tuple(splits)),
        grid=(B, S // tm),
        in_specs=[pl.BlockSpec((1, tm, D), lambda b, i: (b, i, 0)),
                  pl.BlockSpec((1, 6, D), lambda b, i: (b, 0, 0)),
                  _const_spec((1, D)),
                  _const_spec((D, N))],
        out_specs=[pl.BlockSpec((1, tm, wd), lambda b, i: (b, i, 0)) for wd, _ in outs],
        out_shape=[jax.ShapeDtypeStruct((B, S, wd), dt) for wd, dt in outs],
        compiler_params=_cparams(2),
        name="inproj",
    )(x, mod, g.reshape(1, D), w)


def _lru_kernel(ag_ref, ax_ref, cw_ref, cb_ref, wa_ref, ba_ref, wx_ref, bx_ref, lam_ref, o_ref,
                xpad, a_s, u_s, *, S, C, TC):
    PAD = 8
    xpad[0:PAD, :] = jnp.zeros((PAD, C), F32)
    xpad[PAD:PAD + S, :] = ax_ref[0]
    z = -lam_ref[...]
    sp = jnp.maximum(z, 0.0) + jnp.log(1.0 + jnp.exp(-jnp.abs(z)))
    for ci in range(S // TC):
        r0 = ci * TC
        xc = cb_ref[...]
        for k in range(CONV_WIDTH):
            off = PAD - (CONV_WIDTH - 1) + k + r0
            xc = xc + xpad[off:off + TC, :] * cw_ref[k:k + 1, :]
        xb = xc.astype(BF16)
        r = jax.nn.sigmoid(_dot(xb, wa_ref[...]) + ba_ref[...])
        i = jax.nn.sigmoid(_dot(xb, wx_ref[...]) + bx_ref[...])
        log_a = -LRU_C * r * sp
        a = jnp.exp(log_a)
        mult = jnp.sqrt(1.0 - jnp.exp(2.0 * log_a))
        if ci == 0:
            row = lax.broadcasted_iota(jnp.int32, (TC, C), 0)
            mult = jnp.where(row == 0, 1.0, mult)
        a_s[r0:r0 + TC, :] = a
        u_s[r0:r0 + TC, :] = mult * (i * xc)

    row8 = lax.broadcasted_iota(jnp.int32, (8, C), 0)

    def tile(ti, h):
        t0 = pl.multiple_of(ti * 8, 8)
        A = a_s[pl.ds(t0, 8), :]
        U = u_s[pl.ds(t0, 8), :]
        for d in (1, 2, 4):
            As = pltpu.roll(A, d, 0)
            Us = pltpu.roll(U, d, 0)
            ok = row8 >= d
            U = jnp.where(ok, A * Us + U, U)
            A = jnp.where(ok, A * As, A)
        H = U + A * h
        u_s[pl.ds(t0, 8), :] = H
        return H[7:8, :]

    lax.fori_loop(0, S // 8, tile, jnp.zeros((1, C), F32))

    for ci in range(S // TC):
        r0 = ci * TC
        o_ref[0, r0:r0 + TC, :] = (u_s[r0:r0 + TC, :]
                                   * jax.nn.gelu(ag_ref[0, r0:r0 + TC, :])).astype(o_ref.dtype)


def _lru(ag, ax, cw, cb, wa_bd, ba, wx_bd, bx, lam):
    B, S, C = ag.shape
    TC = 256
    row = lambda v: v.reshape(1, C)
    return pl.pallas_call(
        functools.partial(_lru_kernel, S=S, C=C, TC=TC),
        grid=(B,),
        in_specs=[pl.BlockSpec((1, S, C), lambda b: (b, 0, 0)),
                  pl.BlockSpec((1, S, C), lambda b: (b, 0, 0)),
                  _const_spec((CONV_WIDTH, C)), _const_spec((1, C)),
                  _const_spec((C, C)), _const_spec((1, C)),
                  _const_spec((C, C)), _const_spec((1, C)), _const_spec((1, C))],
        out_specs=pl.BlockSpec((1, S, C), lambda b: (b, 0, 0)),
        out_shape=jax.ShapeDtypeStruct((B, S, C), BF16),
        scratch_shapes=[pltpu.VMEM((S + 8, C), F32), pltpu.VMEM((S, C), F32), pltpu.VMEM((S, C), F32)],
        compiler_params=_cparams(1),
        name="rglru",
    )(ag, ax, cw, row(cb), wa_bd, row(ba), wx_bd, row(bx), row(lam))


def _compress_kernel(xk_ref, xv_ref, pek_ref, w1k_ref, w2k_ref, pev_ref, w1v_ref, w2v_ref,
                     ok_ref, ov_ref):
    def one(x_ref, pe_ref, w1_ref, w2_ref, o_ref):
        x = x_ref[0]
        p0 = _dot((x + pe_ref[0:1, :]).astype(BF16), w1_ref[0])
        p1 = _dot((x + pe_ref[1:2, :]).astype(BF16), w1_ref[1])
        nc = x.shape[0]
        hid = p0 + pltpu.roll(p1, nc - 1, 0)
        o_ref[0] = _dot(jax.nn.gelu(hid).astype(BF16), w2_ref[...]).astype(o_ref.dtype)

    one(xk_ref, pek_ref, w1k_ref, w2k_ref, ok_ref)
    one(xv_ref, pev_ref, w1v_ref, w2v_ref, ov_ref)


def _compress(xk, xv, pek, w1k, w2k, pev, w1v, w2v):
    B, NC, W = xk.shape
    GH = NSA_KV_GROUPS * CMP_HIDDEN
    GD = NSA_KV_GROUPS * HEAD_DIM
    xspec = pl.BlockSpec((1, NC, W), lambda b: (b, 0, 0))
    ospec = pl.BlockSpec((1, NC, GD), lambda b: (b, 0, 0))
    return pl.pallas_call(
        _compress_kernel,
        grid=(B,),
        in_specs=[xspec, xspec,
                  _const_spec((2, W)), _const_spec((2, W, GH)), _const_spec((GH, GD)),
                  _const_spec((2, W)), _const_spec((2, W, GH)), _const_spec((GH, GD))],
        out_specs=[ospec, ospec],
        out_shape=[jax.ShapeDtypeStruct((B, NC, GD), BF16)] * 2,
        compiler_params=_cparams(1),
        name="nsa_compress",
    )(xk, xv, pek, w1k, w2k, pev, w1v, w2v)


def _causal_buckets(S, step):
    step = min(S, step)
    assert S % step == 0
    return tuple(range(step, S + 1, step))


def _for_bucket(needed, buckets, body):
    lo = 0
    for lk in buckets:
        pl.when((needed > lo) & (needed <= lk))(functools.partial(body, lk))
        lo = lk


def _eye(QB):
    return (lax.broadcasted_iota(jnp.int32, (QB, LANES), 0)
            == lax.broadcasted_iota(jnp.int32, (QB, LANES), 1)).astype(BF16)


def _q_rows(slot, keep, eye):
    return jnp.concatenate([jnp.where(keep, slot, jnp.zeros((), BF16)), eye], axis=1)


def _colsum(x):
    n, q = x.shape
    part = jnp.sum(x.reshape(n // 64, 64, q), axis=0)
    return jnp.sum(part, axis=0, keepdims=True)


def _colreduce(x, op):
    n, q = x.shape
    return op(op(x.reshape(n // 64, 64, q), axis=0), axis=0, keepdims=True)


def _f32_to_rank(x):
    bits = pltpu.bitcast(x, jnp.int32)
    key = bits ^ (lax.shift_right_arithmetic(bits, 31) & 0x7FFFFFFF)
    return jnp.where(key >= FLT_MIN_BITS, key - (FLT_MIN_BITS - 1), jnp.where(key < -FLT_MIN_BITS, key + FLT_MIN_BITS, 0))


def _rank_to_f32(r):
    key = jnp.where(r > 0, r + (FLT_MIN_BITS - 1), jnp.where(r < 0, r - FLT_MIN_BITS, 0))
    return pltpu.bitcast(key ^ (lax.shift_right_arithmetic(key, 31) & 0x7FFFFFFF), F32)


def _kth_largest(x, x_valid_min, n_valid, k):
    few = n_valid <= k
    lo0 = _f32_to_rank(_colreduce(x_valid_min, jnp.min))
    hi0 = _f32_to_rank(_colreduce(x, jnp.max)) + 1
    done0 = jnp.where(few | (lo0 + 1 == hi0), 1, 0).astype(jnp.int32)
    FIRST_STEPS, CHECK_EVERY, VALUE_STEPS, MAX_STEPS = 20, 4, 24, 64

    def step(u, state):
        it, lo, hi, lof, hif, done = state
        mid_f = 0.5 * lof + 0.5 * hif
        mid_v = _f32_to_rank(mid_f)
        mid_k = lax.shift_right_arithmetic(lo, 1) + lax.shift_right_arithmetic(hi, 1) + (lo & hi & 1)
        use_v = (mid_v > lo) & (mid_v < hi) & (it < VALUE_STEPS)
        near0 = jnp.where((lo < 0) & (hi > 0), 0, jnp.where(lo == 0, 1, -1))
        zero_step = ((lo < 0) & (hi > 0)) | (lo == 0) | (hi == 0)
        mid_k = jnp.where(zero_step, near0, mid_k)
        use_v = use_v & jnp.logical_not(zero_step)
        mid = jnp.where(use_v, mid_v, mid_k)
        midf = jnp.where(use_v, mid_f, _rank_to_f32(mid_k))
        cnt = _colsum(jnp.where(x >= midf, 1.0, 0.0))
        up = (done == 0) & (cnt >= k)
        down = (done == 0) & (cnt < k)
        lo, lof = jnp.where(up, mid, lo), jnp.where(up, midf, lof)
        hi, hif = jnp.where(down, mid, hi), jnp.where(down, midf, hif)
        done = jnp.where((cnt == k) | (lo + 1 == hi), 1, done)
        return it + 1, lo, hi, lof, hif, done

    def body(carry):
        state = lax.fori_loop(0, CHECK_EVERY, step, carry[0:6])
        return (*state, jnp.min(state[5].astype(F32)))

    def cond(carry):
        return (carry[0] < MAX_STEPS) & (carry[6] < 0.5)

    state = lax.fori_loop(0, FIRST_STEPS, step, (jnp.int32(0), lo0, hi0, _rank_to_f32(lo0), _rank_to_f32(hi0), done0))
    lof = lax.while_loop(cond, body, (*state, jnp.min(state[5].astype(F32))))[3]
    return jnp.where(few, -jnp.inf, lof)


def _attend(qa, kaug, vaug):
    s = _dot_nt(qa, kaug)
    m = jnp.max(s, axis=-1, keepdims=True)
    oa = _dot(jnp.exp2(s - m).astype(BF16), vaug)
    return oa[:, 0:LANES], oa[:, LANES:2 * LANES]


def _nsa_body(LK, sub, q_ref, gt_ref, kc_ref, vc_ref, kv_ref, ovt_ref, expt_ref, o_ref, *, QB, NC, NSEL, NTOP):
    t0 = (pl.program_id(0) * ATT_SUB + sub) * QB
    rows = slice(sub * QB, (sub + 1) * QB)
    R = NSA_HPG
    G = NSA_KV_GROUPS
    low = lax.broadcasted_iota(jnp.int32, (QB, LANES), 1) < HEAD_DIM
    keep_g = (low, jnp.logical_not(low))
    eye = _eye(QB)
    q = (q_ref[0, rows, :].astype(F32) * QK_SCALE).astype(BF16)
    gates = jax.nn.sigmoid(gt_ref[0, rows, :])
    qa_g = [jnp.concatenate([_q_rows(q[:, r * LANES:(r + 1) * LANES], keep_g[g], eye) for r in range(R)], axis=0)
            for g in range(G)]
    qa = jnp.concatenate(qa_g, axis=0)

    def tq_t(n):
        return t0 + lax.broadcasted_iota(jnp.int32, (n, QB), 1)

    def vaug(v):
        return jnp.concatenate([v, jnp.ones(v.shape, BF16)], axis=1)

    cend = lax.broadcasted_iota(jnp.int32, (NC, QB), 0) * CMP_STRIDE + (CMP_LEN - 1)
    bias_c = jnp.where(cend <= tq_t(NC), 0.0, NEG).astype(BF16)
    s = _dot_nt(qa, jnp.concatenate([kc_ref[0], bias_c], axis=1))
    m = jnp.max(s, axis=-1, keepdims=True)
    p = jnp.exp2(s - m)
    inv = jnp.where(m > 0.5 * NEG, 1.0 / jnp.maximum(jnp.sum(p, axis=-1, keepdims=True), 1e-30), 0.0)
    o_cmp = _dot(p.astype(BF16), vc_ref[0]) * inv
    pn = p * inv

    start = pl.multiple_of(jnp.maximum(t0 - WIN, 0), QB)
    kpos_w = start + lax.broadcasted_iota(jnp.int32, (WIN + QB, QB), 0)
    bias_w = jnp.where((kpos_w <= tq_t(WIN + QB)) & (kpos_w > tq_t(WIN + QB) - WIN), 0.0, NEG).astype(BF16)
    kwin = kv_ref[0, pl.ds(start, WIN + QB), 2 * LANES:3 * LANES]
    vwin = kv_ref[0, pl.ds(start, WIN + QB), 3 * LANES:4 * LANES]
    ow, lw = _attend(qa, jnp.concatenate([kwin, bias_w], axis=1), vaug(vwin))
    o_win = ow / lw

    jrow = lax.broadcasted_iota(jnp.int32, (LANES, QB), 0)
    cur = lax.shift_right_logical(tq_t(LANES), 6)
    forced = (jrow == 0) | (jrow == cur) | (jrow == cur - 1)
    causal_s = lax.broadcasted_iota(jnp.int32, (LK, QB), 0) <= tq_t(LK)
    kslc = kv_ref[0, 0:LK, 0:LANES]
    vslc = vaug(kv_ref[0, 0:LK, LANES:2 * LANES])
    o_slc = []
    for g in range(G):
        base = g * R * QB
        psum = pn[base:base + QB]
        for r in range(1, R):
            psum = psum + pn[base + r * QB:base + (r + 1) * QB]
        hi = psum.astype(BF16)
        lo = (psum - hi.astype(F32)).astype(BF16)
        imp = _dot_nt(ovt_ref[...], hi) + _dot_nt(ovt_ref[...], lo)
        imp = jnp.where(forced, FORCE_SCORE, imp)
        imp = jnp.where(jrow <= cur, imp, -jnp.inf)
        imp = imp[0:NSEL]
        jr = jrow[0:NSEL]
        cnt = jnp.zeros((NSEL, QB), F32)
        for j2 in range(NSEL):
            rowv = imp[j2:j2 + 1, :]
            cnt = cnt + jnp.where(jr > j2, jnp.where(rowv >= imp, 1.0, 0.0), jnp.where(rowv > imp, 1.0, 0.0))
        sel_t = jnp.where(cnt < NTOP, 1.0, 0.0)
        if NSEL < LANES:
            sel_t = jnp.concatenate([sel_t, jnp.zeros((LANES - NSEL, QB), F32)], axis=0)
        picked = _dot(expt_ref[0:LK, :], sel_t.astype(BF16))
        bias_s = jnp.where((picked > 0.5) & causal_s, 0.0, NEG).astype(BF16)
        os_, ls = _attend(qa_g[g], jnp.concatenate([kslc, bias_s], axis=1), vslc)
        o_slc.append(os_ / ls)

    for r in range(R):
        acc = jnp.zeros((QB, LANES), F32)
        ra, rb = r * QB, (R + r) * QB
        branches = ((o_cmp[ra:ra + QB], o_cmp[rb:rb + QB]),
                    (o_slc[0][ra:ra + QB], o_slc[1][ra:ra + QB]),
                    (o_win[ra:ra + QB], o_win[rb:rb + QB]))
        for j, (va, vb) in enumerate(branches):
            ca, cb = 3 * r + j, 3 * (R + r) + j
            gate = jnp.where(low, gates[:, ca:ca + 1], gates[:, cb:cb + 1])
            acc = acc + gate * jnp.where(low, va, vb)
        o_ref[0, rows, r * LANES:(r + 1) * LANES] = acc.astype(o_ref.dtype)


def _nsa_kernel(*refs, QB, buckets, **kw):
    needed = (pl.program_id(0) + 1) * ATT_SUB * QB

    def body(lk):
        for sub in range(ATT_SUB):
            _nsa_body(lk, sub, *refs, QB=QB, **kw)

    _for_bucket(needed, buckets, body)


def _nsa(q, gts, kc, vc, kv4):
    B, S, _ = q.shape
    QB = Q_BLOCK
    NC = kc.shape[1]
    NSEL = S // SLC_LEN
    NTOP = min(SLC_TOPN, NSEL)
    TQ = ATT_SUB * QB
    assert NSEL <= LANES and S % TQ == 0 and S >= WIN + QB
    c = np.arange(NC)[None, :] * CMP_STRIDE
    j = np.arange(LANES)[:, None] * SLC_LEN
    valid_c = np.arange(NC)[None, :] < (S - CMP_LEN) // CMP_STRIDE + 1
    ovt = ((c < j + SLC_LEN) & (c + CMP_LEN > j) & valid_c & (np.arange(LANES)[:, None] < NSEL))
    expand_t = (np.arange(S)[:, None] // SLC_LEN) == np.arange(LANES)[None, :]
    GD = NSA_KV_GROUPS * HEAD_DIM
    return pl.pallas_call(
        functools.partial(_nsa_kernel, QB=QB, NC=NC, NSEL=NSEL, NTOP=NTOP, buckets=_causal_buckets(S, 512)),
        grid=(S // TQ, B),
        in_specs=[pl.BlockSpec((1, TQ, NSA_HPG * LANES), lambda i, b: (b, i, 0)),
                  pl.BlockSpec((1, TQ, LANES), lambda i, b: (b, i, 0)),
                  pl.BlockSpec((1, NC, GD), lambda i, b: (b, 0, 0)),
                  pl.BlockSpec((1, NC, GD), lambda i, b: (b, 0, 0)),
                  pl.BlockSpec((1, S, 4 * GD), lambda i, b: (b, 0, 0)),
                  _const_spec((LANES, NC)), _const_spec((S, LANES))],
        out_specs=pl.BlockSpec((1, TQ, NSA_HPG * LANES), lambda i, b: (b, i, 0)),
        out_shape=jax.ShapeDtypeStruct((B, S, NSA_HPG * LANES), BF16),
        compiler_params=_cparams(2),
        name="nsa_attention",
    )(q, gts, kc, vc, kv4, jnp.asarray(ovt, BF16), jnp.asarray(expand_t, BF16))


def _dsa_body(LK, q_ref, k_ref, v_ref, qi_ref, ki_ref, wi_ref, o_ref, *, QB, NKEEP, CH, step0):
    TQ = ATT_SUB * QB
    t0 = (pl.program_id(0) + step0) * TQ
    low_t = lax.broadcasted_iota(jnp.int32, (TQ, LANES), 1) < HEAD_DIM
    zero = jnp.zeros((), BF16)
    causal = (lax.broadcasted_iota(jnp.int32, (LK, TQ), 0)
              <= t0 + lax.broadcasted_iota(jnp.int32, (LK, TQ), 1))

    w_t = (wi_ref[0] * IDX_DIM ** -0.5).T
    qi = qi_ref[0]
    ki = ki_ref[0, 0:LK, :]
    score = None
    for u in range(IDX_HEADS // 2):
        slot = qi[:, u * LANES:(u + 1) * LANES]
        pair = jnp.concatenate([jnp.where(low_t, slot, zero), jnp.where(low_t, zero, slot)], axis=0)
        sc = jnp.maximum(_dot_nt(ki, pair), 0.0)
        term = sc[:, 0:TQ] * w_t[2 * u:2 * u + 1, :] + sc[:, TQ:2 * TQ] * w_t[2 * u + 1:2 * u + 2, :]
        score = term if score is None else score + term

    score = score * IDX_HEADS ** -0.5
    masked = jnp.where(causal, score, -jnp.inf)
    n_valid = t0 + lax.broadcasted_iota(jnp.int32, (1, TQ), 1) + 1
    thr = _kth_largest(masked, jnp.where(causal, score, jnp.inf), n_valid, NKEEP)
    gt = masked > thr
    eq = masked == thr
    need = NKEEP - _colsum(jnp.where(gt, 1.0, 0.0))
    eq_b = jnp.where(eq, 1.0, 0.0).astype(BF16)
    tri = (lax.broadcasted_iota(jnp.int32, (CH, CH), 0)
           > lax.broadcasted_iota(jnp.int32, (CH, CH), 1)).astype(BF16)
    ranks, before = [], jnp.zeros((1, TQ), F32)
    for c in range(LK // CH):
        e = eq_b[c * CH:(c + 1) * CH]
        ranks.append(_dot(tri, e) + before)
        before = before + _colsum(e.astype(F32))
    rank = jnp.concatenate(ranks, axis=0)
    bias = jnp.where((gt | (eq & (rank < need))) & causal, 0.0, NEG).astype(BF16)
    k = k_ref[0, 0:LK, :]
    vaug = jnp.concatenate([v_ref[0, 0:LK, :], jnp.ones((LK, LANES), BF16)], axis=1)

    low = lax.broadcasted_iota(jnp.int32, (QB, LANES), 1) < HEAD_DIM
    not_low = jnp.logical_not(low)
    eye = _eye(QB)
    HALF = DSA_HPG // 2
    for sub in range(ATT_SUB):
        rows_q = slice(sub * QB, (sub + 1) * QB)
        kaug = jnp.concatenate([k, bias[:, sub * QB:(sub + 1) * QB]], axis=1)
        for half in range(2):
            rows = []
            for r in range(half * HALF, (half + 1) * HALF):
                slot = (q_ref[0, rows_q, r * LANES:(r + 1) * LANES].astype(F32) * QK_SCALE).astype(BF16)
                rows += [_q_rows(slot, low, eye), _q_rows(slot, not_low, eye)]
            o, l = _attend(jnp.concatenate(rows, axis=0), kaug, vaug)
            o = o / l
            for i in range(HALF):
                r = half * HALF + i
                o_ref[0, rows_q, r * LANES:(r + 1) * LANES] = jnp.where(
                    low, o[2 * i * QB:(2 * i + 1) * QB], o[(2 * i + 1) * QB:(2 * i + 2) * QB]).astype(o_ref.dtype)


def _dsa_kernel(*refs, QB, buckets, step0, **kw):
    needed = (pl.program_id(0) + step0 + 1) * ATT_SUB * QB
    _for_bucket(needed, buckets, lambda lk: _dsa_body(lk, *refs, QB=QB, step0=step0, **kw))


def _bucket_groups(lengths):
    groups, cur = [], []
    for lk in lengths:
        if cur and sum(cur) + lk > DSA_KEYS_PER_PROGRAM:
            groups.append(tuple(cur))
            cur = []
        cur.append(lk)
    groups.append(tuple(cur))
    return groups


def _dsa(q, k, v, qi, ki, wi):
    B, S, _ = q.shape
    QB = Q_BLOCK
    NKEEP = min(IDX_TOPK_MAX, S // 4)
    GD = DSA_KV_HEADS * HEAD_DIM
    TQ = ATT_SUB * QB
    assert S % TQ == 0
    full = lambda w: pl.BlockSpec((1, S, w), lambda i, b: (b, 0, 0))
    outs, step0 = [], 0
    for buckets in _bucket_groups(_causal_buckets(S, TQ)):
        n = len(buckets)
        blk = lambda w, s0=step0: pl.BlockSpec((1, TQ, w), lambda i, b: (b, i + s0, 0))
        outs.append(pl.pallas_call(
            functools.partial(_dsa_kernel, QB=QB, NKEEP=NKEEP, CH=256, buckets=buckets, step0=step0),
            grid=(n, B),
            in_specs=[blk(DSA_HPG * LANES), full(GD), full(GD), blk(IDX_HEADS * IDX_DIM), full(LANES), blk(LANES)],
            out_specs=pl.BlockSpec((1, TQ, DSA_HPG * LANES), lambda i, b: (b, i, 0)),
            out_shape=jax.ShapeDtypeStruct((B, n * TQ, DSA_HPG * LANES), BF16),
            compiler_params=_cparams(2),
            name="dsa_attention",
        )(q, k, v, qi, ki, wi))
        step0 += n
    return jnp.concatenate(outs, axis=1)


def _mix_ffn_kernel(*refs, n_y, final):
    x_ref, mod_ref, gf_ref = refs[0:3]
    y_refs = refs[3:3 + n_y]
    wo_refs = refs[3 + n_y:3 + 2 * n_y]
    wg_ref, wu_ref, wd_ref = refs[3 + 2 * n_y:6 + 2 * n_y]
    rest = refs[6 + 2 * n_y:]
    o_ref = rest[-1]
    mix = None
    for y_ref, wo_ref in zip(y_refs, wo_refs):
        t = _dot(y_ref[0], wo_ref[...])
        mix = t if mix is None else mix + t
    x1 = x_ref[0] + mod_ref[0, 2:3, :] * mix
    h = (_rms(x1, gf_ref[...]) * (1.0 + mod_ref[0, 4:5, :]) + mod_ref[0, 3:4, :]).astype(BF16)
    gate = _dot(h, wg_ref[...])
    up = _dot(h, wu_ref[...])
    act = (gate * jax.nn.sigmoid(gate) * up).astype(BF16)
    x2 = x1 + mod_ref[0, 5:6, :] * _dot(act, wd_ref[...])
    if final:
        x2 = _rms(x2, rest[0][...])
    o_ref[0] = x2


def _mix_ffn(x, mod, g_ffn, ys, wos, wg, wu, wd, g_final=None, tm=512):
    B, S, D = x.shape
    FF = wg.shape[1]
    final = g_final is not None
    row_blk = lambda w: pl.BlockSpec((1, tm, w), lambda b, i: (b, i, 0))
    in_specs = ([row_blk(D), pl.BlockSpec((1, 6, D), lambda b, i: (b, 0, 0)), _const_spec((1, D))]
                + [row_blk(y.shape[2]) for y in ys]
                + [_const_spec(w.shape) for w in wos]
                + [_const_spec((D, FF)), _const_spec((D, FF)), _const_spec((FF, D))])
    args = [x, mod, g_ffn.reshape(1, D), *ys, *wos, wg, wu, wd]
    if final:
        in_specs.append(_const_spec((1, D)))
        args.append(g_final.reshape(1, D))
    return pl.pallas_call(
        functools.partial(_mix_ffn_kernel, n_y=len(ys), final=final),
        grid=(B, S // tm),
        in_specs=in_specs,
        out_specs=row_blk(D),
        out_shape=jax.ShapeDtypeStruct((B, S, D), F32),
        compiler_params=_cparams(2),
        name="mix_ffn",
    )(*args)


def _block_diag(w):
    nb, bi, bo = w.shape
    eye = jnp.eye(nb, dtype=w.dtype)
    return (w[:, :, None, :] * eye[:, None, :, None]).reshape(nb * bi, nb * bo)


def _pair_heads(n_groups, per_group):
    assert n_groups == 2
    return [g * per_group + r for r in range(per_group) for g in range(n_groups)]


def _head_cols(order):
    return np.concatenate([np.arange(h * HEAD_DIM, (h + 1) * HEAD_DIM) for h in order])


def _pad_cols(w, n):
    return jnp.pad(w, ((0, 0), (0, n - w.shape[1])))


def _compress_weights(pe, w1, w2):
    G, D, H = NSA_KV_GROUPS, HEAD_DIM, CMP_HIDDEN
    half = CMP_LEN // 2
    pe_r = jnp.broadcast_to(pe.reshape(2, half, 1, D), (2, half, G, D)).reshape(2, half * G * D)
    eye = jnp.eye(G, dtype=w1.dtype)
    w1e = (w1.reshape(2, half, 1, D, 1, H) * eye[None, None, :, None, :, None]).reshape(2, half * G * D, G * H)
    w2e = (w2[None, :, None, :] * eye[:, None, :, None]).reshape(G * H, G * D)
    return pe_r, w1e.astype(BF16), w2e.astype(BF16)


def _layer_ab(x, c, norm_mix, norm_ffn, mod_w, mod_b, w_in, conv_w, conv_b, wa, ba, wx, bx, lam,
              pe_k, w1_k, w2_k, pe_v, w1_v, w2_v, w_out, wg, wu, wd):
    B, S, D = x.shape
    mod = _adaln(c, mod_w, mod_b).reshape(B, 6, D)
    C = LRU_WIDTH
    HQ = NSA_HEADS * HEAD_DIM
    GD = NSA_KV_GROUPS * HEAD_DIM
    order = _pair_heads(NSA_KV_GROUPS, NSA_HPG)
    o_q = 2 * C
    o_kv = o_q + HQ
    o_gt = o_kv + 6 * GD
    w_q = w_in[:, o_q:o_kv][:, _head_cols(order)]
    w_all = jnp.concatenate([w_in[:, :o_q], w_q, w_in[:, o_kv:o_gt], _pad_cols(w_in[:, o_gt:], LANES)],
                            axis=1).astype(BF16)
    ag, ax, q, kcmp, vcmp, kv4, gts = _inproj(
        x, mod, norm_mix, w_all,
        [(C, F32), (C, F32), (HQ, BF16), (GD, F32), (GD, F32), (4 * GD, BF16), (LANES, F32)])
    y_a = _lru(ag, ax, conv_w, conv_b, _block_diag(wa).astype(BF16), ba, _block_diag(wx).astype(BF16), bx, lam)
    NC = S // CMP_STRIDE
    kc, vc = _compress(kcmp.reshape(B, NC, CMP_STRIDE * GD), vcmp.reshape(B, NC, CMP_STRIDE * GD),
                       *_compress_weights(pe_k, w1_k, w2_k), *_compress_weights(pe_v, w1_v, w2_v))
    y_b = _nsa(q, gts, kc, vc, kv4)
    wo_a = w_out[:C].astype(BF16)
    wo_b = w_out[C:][_head_cols(order)].astype(BF16)
    return _mix_ffn(x, mod, norm_ffn, [y_a, y_b], [wo_a, wo_b],
                    wg.astype(BF16), wu.astype(BF16), wd.astype(BF16))


def _layer_c(x, c, norm_mix, norm_ffn, mod_w, mod_b, w_in, w_out, wg, wu, wd, final_norm):
    B, S, D = x.shape
    mod = _adaln(c, mod_w, mod_b).reshape(B, 6, D)
    HQ = DSA_HEADS * HEAD_DIM
    GD = DSA_KV_HEADS * HEAD_DIM
    HI = IDX_HEADS * IDX_DIM
    order = _pair_heads(DSA_KV_HEADS, DSA_HPG)
    o_k = HQ
    o_v = o_k + GD
    o_qi = o_v + GD
    o_ki = o_qi + HI
    o_wi = o_ki + IDX_DIM
    w_ki = w_in[:, o_ki:o_wi]
    w_all = jnp.concatenate([w_in[:, :HQ][:, _head_cols(order)], w_in[:, o_k:o_qi], w_in[:, o_qi:o_ki],
                             w_ki, w_ki, _pad_cols(w_in[:, o_wi:], LANES)], axis=1).astype(BF16)
    q, k, v, qi, ki, wi = _inproj(
        x, mod, norm_mix, w_all,
        [(HQ, BF16), (GD, BF16), (GD, BF16), (HI, BF16), (LANES, BF16), (LANES, F32)])
    y = _dsa(q, k, v, qi, ki, wi)
    wo = w_out[_head_cols(order)].astype(BF16)
    return _mix_ffn(x, mod, norm_ffn, [y], [wo], wg.astype(BF16), wu.astype(BF16), wd.astype(BF16),
                    g_final=final_norm)


def kernel(x, c, l0_norm_mix, l0_norm_ffn, l0_mod_w, l0_mod_b, l0_w_in, l0_conv_w, l0_conv_b, l0_lru_wa, l0_lru_ba, l0_lru_wx, l0_lru_bx, l0_lru_lambda, l0_cmp_pe_k, l0_cmp_w1_k, l0_cmp_w2_k, l0_cmp_pe_v, l0_cmp_w1_v, l0_cmp_w2_v, l0_w_out, l0_ffn_wg, l0_ffn_wu, l0_ffn_wd, l1_norm_mix, l1_norm_ffn, l1_mod_w, l1_mod_b, l1_w_in, l1_w_out, l1_ffn_wg, l1_ffn_wu, l1_ffn_wd, final_norm):
    x = _layer_ab(x, c, l0_norm_mix, l0_norm_ffn, l0_mod_w, l0_mod_b, l0_w_in, l0_conv_w, l0_conv_b,
                  l0_lru_wa, l0_lru_ba, l0_lru_wx, l0_lru_bx, l0_lru_lambda,
                  l0_cmp_pe_k, l0_cmp_w1_k, l0_cmp_w2_k, l0_cmp_pe_v, l0_cmp_w1_v, l0_cmp_w2_v,
                  l0_w_out, l0_ffn_wg, l0_ffn_wu, l0_ffn_wd)
    return _layer_c(x, c, l1_norm_mix, l1_norm_ffn, l1_mod_w, l1_mod_b, l1_w_in, l1_w_out,
                    l1_ffn_wg, l1_ffn_wu, l1_ffn_wd, final_norm)
```

```python
import functools

import numpy as np
import jax
import jax.numpy as jnp
from jax import lax
from jax.experimental import pallas as pl
from jax.experimental.pallas import tpu as pltpu

F32 = jnp.float32
BF16 = jnp.bfloat16

EPS = 1e-6
HEAD_DIM = 64
Q_BLOCK = 128
ATT_SUB = 2
NSA_SUB = 4
LRU_WIDTH = 512
LRU_BLOCKS = 8
CONV_WIDTH = 4
LRU_C = 8.0
NSA_HEADS = 8
NSA_KV_GROUPS = 2
NSA_HPG = NSA_HEADS // NSA_KV_GROUPS
CMP_LEN = 32
CMP_STRIDE = 16
CMP_HIDDEN = 128
SLC_LEN = 64
SLC_TOPN = 8
WIN = 512
FORCE_SCORE = 1e4
DSA_HEADS = 16
DSA_KV_HEADS = 2
DSA_HPG = DSA_HEADS // DSA_KV_HEADS
IDX_HEADS = 8
IDX_DIM = 64
IDX_TOPK_MAX = 256

LANES = 128
NEG = -1e30
VMEM_LIMIT = 56 * 1024 * 1024
QK_SCALE = HEAD_DIM ** -0.5 * float(np.log2(np.e))
FLT_MIN_BITS = 0x00800000
DSA_KEYS_PER_PROGRAM = 3584


def _cparams(n_grid):
    return pltpu.CompilerParams(dimension_semantics=("arbitrary",) * n_grid,
                                vmem_limit_bytes=VMEM_LIMIT)


def _const_spec(shape):
    nd = len(shape)
    return pl.BlockSpec(shape, lambda *_: (0,) * nd, pipeline_mode=pl.Buffered(1))


def _dot(a, b):
    return jnp.dot(a, b, preferred_element_type=F32)


def _dot_nt(a, b):
    return lax.dot_general(a, b, (((1,), (1,)), ((), ())), preferred_element_type=F32)


def _rms(x, g):
    return x * lax.rsqrt(jnp.mean(x * x, axis=-1, keepdims=True) + EPS) * g


def _adaln_kernel(c_ref, w_ref, b_ref, o_ref):
    c = c_ref[...]
    a = (c * jax.nn.sigmoid(c)).astype(BF16)
    o_ref[...] = _dot(a, w_ref[...].astype(BF16)) + b_ref[...]


def _adaln(c, w, b):
    B, D = c.shape
    N = w.shape[1]
    tn = N // 4
    return pl.pallas_call(
        _adaln_kernel,
        grid=(N // tn,),
        in_specs=[pl.BlockSpec((B, D), lambda j: (0, 0)),
                  pl.BlockSpec((D, tn), lambda j: (0, j)),
                  pl.BlockSpec((1, tn), lambda j: (0, j))],
        out_specs=pl.BlockSpec((B, tn), lambda j: (0, j)),
        out_shape=jax.ShapeDtypeStruct((B, N), F32),
        compiler_params=_cparams(1),
        name="adaln",
    )(c, w, b.reshape(1, N))


def _inproj_kernel(x_ref, mod_ref, g_ref, w_ref, *o_refs, splits):
    x = x_ref[0]
    h = _rms(x, g_ref[...]) * (1.0 + mod_ref[0, 1:2, :]) + mod_ref[0, 0:1, :]
    acc = _dot(h.astype(BF16), w_ref[...])
    for o_ref, (c0, wd) in zip(o_refs, splits):
        o_ref[0] = acc[:, c0:c0 + wd].astype(o_ref.dtype)


def _inproj(x, mod, g, w, outs, tm=512):
    B, S, D = x.shape
    N = w.shape[1]
    splits, c0 = [], 0
    for wd, _ in outs:
        splits.append((c0, wd))
        c0 += wd
    assert c0 == N
    return pl.pallas_call(
        functools.partial(_inproj_kernel, splits=tuple(splits)),
        grid=(B, S // tm),
        in_specs=[pl.BlockSpec((1, tm, D), lambda b, i: (b, i, 0)),
                  pl.BlockSpec((1, 6, D), lambda b, i: (b, 0, 0)),
                  _const_spec((1, D)),
                  _const_spec((D, N))],
        out_specs=[pl.BlockSpec((1, tm, wd), lambda b, i: (b, i, 0)) for wd, _ in outs],
        out_shape=[jax.ShapeDtypeStruct((B, S, wd), dt) for wd, dt in outs],
        compiler_params=_cparams(2),
        name="inproj",
    )(x, mod, g.reshape(1, D), w)


def _lru_kernel(ag_ref, ax_ref, cw_ref, cb_ref, wa_ref, ba_ref, wx_ref, bx_ref, lam_ref, o_ref,
                xpad, a_s, u_s, *, S, C, TC):
    PAD = 8
    xpad[0:PAD, :] = jnp.zeros((PAD, C), F32)
    xpad[PAD:PAD + S, :] = ax_ref[0]
    z = -lam_ref[...]
    sp = jnp.maximum(z, 0.0) + jnp.log(1.0 + jnp.exp(-jnp.abs(z)))
    for ci in range(S // TC):
        r0 = ci * TC
        xc = cb_ref[...]
        for k in range(CONV_WIDTH):
            off = PAD - (CONV_WIDTH - 1) + k + r0
            xc = xc + xpad[off:off + TC, :] * cw_ref[k:k + 1, :]
        xb = xc.astype(BF16)
        r = jax.nn.sigmoid(_dot(xb, wa_ref[...]) + ba_ref[...])
        i = jax.nn.sigmoid(_dot(xb, wx_ref[...]) + bx_ref[...])
        log_a = -LRU_C * r * sp
        a = jnp.exp(log_a)
        mult = jnp.sqrt(1.0 - a * a)
        if ci == 0:
            row = lax.broadcasted_iota(jnp.int32, (TC, C), 0)
            mult = jnp.where(row == 0, 1.0, mult)
        a_s[r0:r0 + TC, :] = a
        u_s[r0:r0 + TC, :] = mult * (i * xc)

    row8 = lax.broadcasted_iota(jnp.int32, (8, C), 0)

    def tile(ti, h):
        t0 = pl.multiple_of(ti * 8, 8)
        A = a_s[pl.ds(t0, 8), :]
        U = u_s[pl.ds(t0, 8), :]
        for d in (1, 2, 4):
            As = pltpu.roll(A, d, 0)
            Us = pltpu.roll(U, d, 0)
            ok = row8 >= d
            U = jnp.where(ok, A * Us + U, U)
            A = jnp.where(ok, A * As, A)
        H = U + A * h
        u_s[pl.ds(t0, 8), :] = H
        return H[7:8, :]

    lax.fori_loop(0, S // 8, tile, jnp.zeros((1, C), F32))

    for ci in range(S // TC):
        r0 = ci * TC
        o_ref[0, r0:r0 + TC, :] = (u_s[r0:r0 + TC, :]
                                   * jax.nn.gelu(ag_ref[0, r0:r0 + TC, :])).astype(o_ref.dtype)


def _lru(ag, ax, cw, cb, wa_bd, ba, wx_bd, bx, lam):
    B, S, C = ag.shape
    TC = 256
    row = lambda v: v.reshape(1, C)
    return pl.pallas_call(
        functools.partial(_lru_kernel, S=S, C=C, TC=TC),
        grid=(B,),
        in_specs=[pl.BlockSpec((1, S, C), lambda b: (b, 0, 0)),
                  pl.BlockSpec((1, S, C), lambda b: (b, 0, 0)),
                  _const_spec((CONV_WIDTH, C)), _const_spec((1, C)),
                  _const_spec((C, C)), _const_spec((1, C)),
                  _const_spec((C, C)), _const_spec((1, C)), _const_spec((1, C))],
        out_specs=pl.BlockSpec((1, S, C), lambda b: (b, 0, 0)),
        out_shape=jax.ShapeDtypeStruct((B, S, C), BF16),
        scratch_shapes=[pltpu.VMEM((S + 8, C), F32), pltpu.VMEM((S, C), F32), pltpu.VMEM((S, C), F32)],
        compiler_params=_cparams(1),
        name="rglru",
    )(ag, ax, cw, row(cb), wa_bd, row(ba), wx_bd, row(bx), row(lam))


def _compress_kernel(xk_ref, xv_ref, pek_ref, w1k_ref, w2k_ref, pev_ref, w1v_ref, w2v_ref,
                     ok_ref, ov_ref):
    def one(x_ref, pe_ref, w1_ref, w2_ref, o_ref):
        x = x_ref[0]
        p0 = _dot((x + pe_ref[0:1, :]).astype(BF16), w1_ref[0])
        p1 = _dot((x + pe_ref[1:2, :]).astype(BF16), w1_ref[1])
        nc = x.shape[0]
        hid = p0 + pltpu.roll(p1, nc - 1, 0)
        o_ref[0] = _dot(jax.nn.gelu(hid).astype(BF16), w2_ref[...]).astype(o_ref.dtype)

    one(xk_ref, pek_ref, w1k_ref, w2k_ref, ok_ref)
    one(xv_ref, pev_ref, w1v_ref, w2v_ref, ov_ref)


def _compress(xk, xv, pek, w1k, w2k, pev, w1v, w2v):
    B, NC, W = xk.shape
    GH = NSA_KV_GROUPS * CMP_HIDDEN
    GD = NSA_KV_GROUPS * HEAD_DIM
    xspec = pl.BlockSpec((1, NC, W), lambda b: (b, 0, 0))
    ospec = pl.BlockSpec((1, NC, GD), lambda b: (b, 0, 0))
    return pl.pallas_call(
        _compress_kernel,
        grid=(B,),
        in_specs=[xspec, xspec,
                  _const_spec((2, W)), _const_spec((2, W, GH)), _const_spec((GH, GD)),
                  _const_spec((2, W)), _const_spec((2, W, GH)), _const_spec((GH, GD))],
        out_specs=[ospec, ospec],
        out_shape=[jax.ShapeDtypeStruct((B, NC, GD), BF16)] * 2,
        compiler_params=_cparams(1),
        name="nsa_compress",
    )(xk, xv, pek, w1k, w2k, pev, w1v, w2v)


def _causal_buckets(S, step):
    step = min(S, step)
    assert S % step == 0
    return tuple(range(step, S + 1, step))


def _for_bucket(needed, buckets, body):
    lo = 0
    for lk in buckets:
        pl.when((needed > lo) & (needed <= lk))(functools.partial(body, lk))
        lo = lk


def _eye(QB):
    return (lax.broadcasted_iota(jnp.int32, (QB, LANES), 0)
            == lax.broadcasted_iota(jnp.int32, (QB, LANES), 1)).astype(BF16)


def _q_rows(slot, keep, eye):
    return jnp.concatenate([jnp.where(keep, slot, jnp.zeros((), BF16)), eye], axis=1)


def _colsum(x):
    n, q = x.shape
    part = jnp.sum(x.reshape(n // 64, 64, q), axis=0)
    return jnp.sum(part, axis=0, keepdims=True)


def _colreduce(x, op):
    n, q = x.shape
    return op(op(x.reshape(n // 64, 64, q), axis=0), axis=0, keepdims=True)


def _f32_to_rank(x):
    bits = pltpu.bitcast(x, jnp.int32)
    key = bits ^ (lax.shift_right_arithmetic(bits, 31) & 0x7FFFFFFF)
    return jnp.where(key >= FLT_MIN_BITS, key - (FLT_MIN_BITS - 1), jnp.where(key < -FLT_MIN_BITS, key + FLT_MIN_BITS, 0))


def _rank_to_f32(r):
    key = jnp.where(r > 0, r + (FLT_MIN_BITS - 1), jnp.where(r < 0, r - FLT_MIN_BITS, 0))
    return pltpu.bitcast(key ^ (lax.shift_right_arithmetic(key, 31) & 0x7FFFFFFF), F32)


def _kth_largest(x, x_valid_min, n_valid, k):
    few = n_valid <= k
    lo0 = _f32_to_rank(_colreduce(x_valid_min, jnp.min))
    hi0 = _f32_to_rank(_colreduce(x, jnp.max)) + 1
    done0 = jnp.where(few | (lo0 + 1 == hi0), 1, 0).astype(jnp.int32)
    FIRST_STEPS, CHECK_EVERY, VALUE_STEPS, MAX_STEPS = 20, 4, 24, 64

    def step(u, state):
        it, lo, hi, lof, hif, done = state
        mid_f = 0.5 * lof + 0.5 * hif
        mid_v = _f32_to_rank(mid_f)
        mid_k = lax.shift_right_arithmetic(lo, 1) + lax.shift_right_arithmetic(hi, 1) + (lo & hi & 1)
        use_v = (mid_v > lo) & (mid_v < hi) & (it < VALUE_STEPS)
        near0 = jnp.where((lo < 0) & (hi > 0), 0, jnp.where(lo == 0, 1, -1))
        zero_step = ((lo < 0) & (hi > 0)) | (lo == 0) | (hi == 0)
        mid_k = jnp.where(zero_step, near0, mid_k)
        use_v = use_v & jnp.logical_not(zero_step)
        mid = jnp.where(use_v, mid_v, mid_k)
        midf = jnp.where(use_v, mid_f, _rank_to_f32(mid_k))
        cnt = _colsum(jnp.where(x >= midf, 1.0, 0.0))
        up = (done == 0) & (cnt >= k)
        down = (done == 0) & (cnt < k)
        lo, lof = jnp.where(up, mid, lo), jnp.where(up, midf, lof)
        hi, hif = jnp.where(down, mid, hi), jnp.where(down, midf, hif)
        done = jnp.where((cnt == k) | (lo + 1 == hi), 1, done)
        return it + 1, lo, hi, lof, hif, done

    def body(carry):
        state = lax.fori_loop(0, CHECK_EVERY, step, carry[0:6])
        return (*state, jnp.min(state[5].astype(F32)))

    def cond(carry):
        return (carry[0] < MAX_STEPS) & (carry[6] < 0.5)

    state = lax.fori_loop(0, FIRST_STEPS, step, (jnp.int32(0), lo0, hi0, _rank_to_f32(lo0), _rank_to_f32(hi0), done0))
    lof = lax.while_loop(cond, body, (*state, jnp.min(state[5].astype(F32))))[3]
    return jnp.where(few, -jnp.inf, lof)


def _attend(qa, kaug, vaug):
    s = _dot_nt(qa, kaug)
    m = jnp.max(s, axis=-1, keepdims=True)
    oa = _dot(jnp.exp2(s - m).astype(BF16), vaug)
    return oa[:, 0:LANES], oa[:, LANES:2 * LANES]


def _nsa_body(LK, sub, q_ref, gt_ref, kc_ref, vc_ref, kv_ref, ovt_ref, expt_ref, o_ref, *, QB, NC, NSEL, NTOP):
    t0 = LK - (NSA_SUB - sub) * QB
    rows = slice(sub * QB, (sub + 1) * QB)
    R = NSA_HPG
    G = NSA_KV_GROUPS
    low = lax.broadcasted_iota(jnp.int32, (QB, LANES), 1) < HEAD_DIM
    keep_g = (low, jnp.logical_not(low))
    eye = _eye(QB)
    q = (q_ref[0, rows, :].astype(F32) * QK_SCALE).astype(BF16)
    gates = jax.nn.sigmoid(gt_ref[0, rows, :])
    qa_g = [jnp.concatenate([_q_rows(q[:, r * LANES:(r + 1) * LANES], keep_g[g], eye) for r in range(R)], axis=0)
            for g in range(G)]
    qa = jnp.concatenate(qa_g, axis=0)

    def tq_t(n):
        return t0 + lax.broadcasted_iota(jnp.int32, (n, QB), 1)

    def vaug(v):
        return jnp.concatenate([v, jnp.ones(v.shape, BF16)], axis=1)

    cend = lax.broadcasted_iota(jnp.int32, (NC, QB), 0) * CMP_STRIDE + (CMP_LEN - 1)
    bias_c = jnp.where(cend <= tq_t(NC), 0.0, NEG).astype(BF16)
    s = _dot_nt(qa, jnp.concatenate([kc_ref[0], bias_c], axis=1))
    m = jnp.max(s, axis=-1, keepdims=True)
    p = jnp.exp2(s - m)
    inv = jnp.where(m > 0.5 * NEG, 1.0 / jnp.maximum(jnp.sum(p, axis=-1, keepdims=True), 1e-30), 0.0)
    o_cmp = _dot(p.astype(BF16), vc_ref[0]) * inv
    pn = p * inv

    start = max(t0 - WIN, 0)
    nwin = t0 + QB - start
    kpos_w = start + lax.broadcasted_iota(jnp.int32, (nwin, QB), 0)
    bias_w = jnp.where((kpos_w <= tq_t(nwin)) & (kpos_w > tq_t(nwin) - WIN), 0.0, NEG).astype(BF16)
    kwin = kv_ref[0, start:start + nwin, 2 * LANES:3 * LANES]
    vwin = kv_ref[0, start:start + nwin, 3 * LANES:4 * LANES]
    ow, lw = _attend(qa, jnp.concatenate([kwin, bias_w], axis=1), vaug(vwin))
    o_win = ow / lw

    jrow = lax.broadcasted_iota(jnp.int32, (LANES, QB), 0)
    cur = lax.shift_right_logical(tq_t(LANES), 6)
    forced = (jrow == 0) | (jrow == cur) | (jrow == cur - 1)
    causal_s = lax.broadcasted_iota(jnp.int32, (LK, QB), 0) <= tq_t(LK)
    kslc = kv_ref[0, 0:LK, 0:LANES]
    vslc = vaug(kv_ref[0, 0:LK, LANES:2 * LANES])
    o_slc = []
    for g in range(G):
        base = g * R * QB
        psum = pn[base:base + QB]
        for r in range(1, R):
            psum = psum + pn[base + r * QB:base + (r + 1) * QB]
        hi = psum.astype(BF16)
        lo = (psum - hi.astype(F32)).astype(BF16)
        imp = _dot_nt(ovt_ref[...], hi) + _dot_nt(ovt_ref[...], lo)
        imp = jnp.where(forced, FORCE_SCORE, imp)
        imp = jnp.where(jrow <= cur, imp, -jnp.inf)
        imp = imp[0:NSEL]
        jr = jrow[0:NSEL]
        cnt = jnp.zeros((NSEL, QB), F32)
        for j2 in range(NSEL):
            rowv = imp[j2:j2 + 1, :]
            cnt = cnt + jnp.where(jr > j2, jnp.where(rowv >= imp, 1.0, 0.0), jnp.where(rowv > imp, 1.0, 0.0))
        sel_t = jnp.where(cnt < NTOP, 1.0, 0.0)
        if NSEL < LANES:
            sel_t = jnp.concatenate([sel_t, jnp.zeros((LANES - NSEL, QB), F32)], axis=0)
        picked = _dot(expt_ref[0:LK, :], sel_t.astype(BF16))
        bias_s = jnp.where((picked > 0.5) & causal_s, 0.0, NEG).astype(BF16)
        os_, ls = _attend(qa_g[g], jnp.concatenate([kslc, bias_s], axis=1), vslc)
        o_slc.append(os_ / ls)

    for r in range(R):
        acc = jnp.zeros((QB, LANES), F32)
        ra, rb = r * QB, (R + r) * QB
        branches = ((o_cmp[ra:ra + QB], o_cmp[rb:rb + QB]),
                    (o_slc[0][ra:ra + QB], o_slc[1][ra:ra + QB]),
                    (o_win[ra:ra + QB], o_win[rb:rb + QB]))
        for j, (va, vb) in enumerate(branches):
            ca, cb = 3 * r + j, 3 * (R + r) + j
            gate = jnp.where(low, gates[:, ca:ca + 1], gates[:, cb:cb + 1])
            acc = acc + gate * jnp.where(low, va, vb)
        o_ref[0, rows, r * LANES:(r + 1) * LANES] = acc.astype(o_ref.dtype)


def _nsa_kernel(*refs, LK, **kw):
    for sub in range(NSA_SUB):
        _nsa_body(LK, sub, *refs, **kw)


def _nsa(q, gts, kc, vc, kv4):
    B, S, _ = q.shape
    QB = Q_BLOCK
    NC = kc.shape[1]
    NSEL = S // SLC_LEN
    NTOP = min(SLC_TOPN, NSEL)
    TQ = NSA_SUB * QB
    assert NSEL <= LANES and S % TQ == 0 and S >= WIN + QB
    c = np.arange(NC)[None, :] * CMP_STRIDE
    j = np.arange(LANES)[:, None] * SLC_LEN
    valid_c = np.arange(NC)[None, :] < (S - CMP_LEN) // CMP_STRIDE + 1
    ovt = ((c < j + SLC_LEN) & (c + CMP_LEN > j) & valid_c & (np.arange(LANES)[:, None] < NSEL))
    expand_t = (np.arange(S)[:, None] // SLC_LEN) == np.arange(LANES)[None, :]
    GD = NSA_KV_GROUPS * HEAD_DIM
    ovt, expand_t = jnp.asarray(ovt, BF16), jnp.asarray(expand_t, BF16)
    outs = []
    for step in range(S // TQ):
        blk = lambda w, s=step: pl.BlockSpec((1, TQ, w), lambda b: (b, s, 0))
        outs.append(pl.pallas_call(
            functools.partial(_nsa_kernel, LK=(step + 1) * TQ, QB=QB, NC=NC, NSEL=NSEL, NTOP=NTOP),
            grid=(B,),
            in_specs=[blk(NSA_HPG * LANES), blk(LANES),
                      pl.BlockSpec((1, NC, GD), lambda b: (b, 0, 0)),
                      pl.BlockSpec((1, NC, GD), lambda b: (b, 0, 0)),
                      pl.BlockSpec((1, S, 4 * GD), lambda b: (b, 0, 0)),
                      _const_spec((LANES, NC)), _const_spec((S, LANES))],
            out_specs=pl.BlockSpec((1, TQ, NSA_HPG * LANES), lambda b: (b, 0, 0)),
            out_shape=jax.ShapeDtypeStruct((B, TQ, NSA_HPG * LANES), BF16),
            compiler_params=_cparams(1),
            name="nsa_attention",
        )(q, gts, kc, vc, kv4, ovt, expand_t))
    return jnp.concatenate(outs, axis=1)


def _dsa_body(LK, q_ref, k_ref, v_ref, qi_ref, ki_ref, wi_ref, o_ref, *, QB, NKEEP, CH, step0):
    TQ = ATT_SUB * QB
    t0 = (pl.program_id(0) + step0) * TQ
    low_t = lax.broadcasted_iota(jnp.int32, (TQ, LANES), 1) < HEAD_DIM
    zero = jnp.zeros((), BF16)
    causal = (lax.broadcasted_iota(jnp.int32, (LK, TQ), 0)
              <= t0 + lax.broadcasted_iota(jnp.int32, (LK, TQ), 1))

    w_t = (wi_ref[0] * IDX_DIM ** -0.5).T
    qi = qi_ref[0]
    ki = ki_ref[0, 0:LK, :]
    score = None
    for u in range(IDX_HEADS // 2):
        slot = qi[:, u * LANES:(u + 1) * LANES]
        pair = jnp.concatenate([jnp.where(low_t, slot, zero), jnp.where(low_t, zero, slot)], axis=0)
        sc = jnp.maximum(_dot_nt(ki, pair), 0.0)
        term = sc[:, 0:TQ] * w_t[2 * u:2 * u + 1, :] + sc[:, TQ:2 * TQ] * w_t[2 * u + 1:2 * u + 2, :]
        score = term if score is None else score + term

    score = score * IDX_HEADS ** -0.5
    masked = jnp.where(causal, score, -jnp.inf)
    n_valid = t0 + lax.broadcasted_iota(jnp.int32, (1, TQ), 1) + 1
    thr = _kth_largest(masked, jnp.where(causal, score, jnp.inf), n_valid, NKEEP)
    gt = masked > thr
    eq = masked == thr
    need = NKEEP - _colsum(jnp.where(gt, 1.0, 0.0))
    eq_b = jnp.where(eq, 1.0, 0.0).astype(BF16)
    tri = (lax.broadcasted_iota(jnp.int32, (CH, CH), 0)
           > lax.broadcasted_iota(jnp.int32, (CH, CH), 1)).astype(BF16)
    ranks, before = [], jnp.zeros((1, TQ), F32)
    for c in range(LK // CH):
        e = eq_b[c * CH:(c + 1) * CH]
        ranks.append(_dot(tri, e) + before)
        before = before + _colsum(e.astype(F32))
    rank = jnp.concatenate(ranks, axis=0)
    bias = jnp.where((gt | (eq & (rank < need))) & causal, 0.0, NEG).astype(BF16)
    k = k_ref[0, 0:LK, :]
    vaug = jnp.concatenate([v_ref[0, 0:LK, :], jnp.ones((LK, LANES), BF16)], axis=1)

    low = lax.broadcasted_iota(jnp.int32, (QB, LANES), 1) < HEAD_DIM
    not_low = jnp.logical_not(low)
    eye = _eye(QB)
    HALF = DSA_HPG // 2
    for sub in range(ATT_SUB):
        rows_q = slice(sub * QB, (sub + 1) * QB)
        kaug = jnp.concatenate([k, bias[:, sub * QB:(sub + 1) * QB]], axis=1)
        for half in range(2):
            rows = []
            for r in range(half * HALF, (half + 1) * HALF):
                slot = (q_ref[0, rows_q, r * LANES:(r + 1) * LANES].astype(F32) * QK_SCALE).astype(BF16)
                rows += [_q_rows(slot, low, eye), _q_rows(slot, not_low, eye)]
            o, l = _attend(jnp.concatenate(rows, axis=0), kaug, vaug)
            o = o / l
            for i in range(HALF):
                r = half * HALF + i
                o_ref[0, rows_q, r * LANES:(r + 1) * LANES] = jnp.where(
                    low, o[2 * i * QB:(2 * i + 1) * QB], o[(2 * i + 1) * QB:(2 * i + 2) * QB]).astype(o_ref.dtype)


def _dsa_kernel(*refs, QB, buckets, step0, **kw):
    needed = (pl.program_id(0) + step0 + 1) * ATT_SUB * QB
    _for_bucket(needed, buckets, lambda lk: _dsa_body(lk, *refs, QB=QB, step0=step0, **kw))


def _bucket_groups(lengths):
    groups, cur = [], []
    for lk in lengths:
        if cur and sum(cur) + lk > DSA_KEYS_PER_PROGRAM:
            groups.append(tuple(cur))
            cur = []
        cur.append(lk)
    groups.append(tuple(cur))
    return groups


def _dsa(q, k, v, qi, ki, wi):
    B, S, _ = q.shape
    QB = Q_BLOCK
    NKEEP = min(IDX_TOPK_MAX, S // 4)
    GD = DSA_KV_HEADS * HEAD_DIM
    TQ = ATT_SUB * QB
    assert S % TQ == 0
    full = lambda w: pl.BlockSpec((1, S, w), lambda i, b: (b, 0, 0))
    outs, step0 = [], 0
    for buckets in _bucket_groups(_causal_buckets(S, TQ)):
        n = len(buckets)
        blk = lambda w, s0=step0: pl.BlockSpec((1, TQ, w), lambda i, b: (b, i + s0, 0))
        outs.append(pl.pallas_call(
            functools.partial(_dsa_kernel, QB=QB, NKEEP=NKEEP, CH=256, buckets=buckets, step0=step0),
            grid=(n, B),
            in_specs=[blk(DSA_HPG * LANES), full(GD), full(GD), blk(IDX_HEADS * IDX_DIM), full(LANES), blk(LANES)],
            out_specs=pl.BlockSpec((1, TQ, DSA_HPG * LANES), lambda i, b: (b, i, 0)),
            out_shape=jax.ShapeDtypeStruct((B, n * TQ, DSA_HPG * LANES), BF16),
            compiler_params=_cparams(2),
            name="dsa_attention",
        )(q, k, v, qi, ki, wi))
        step0 += n
    return jnp.concatenate(outs, axis=1)


def _mix_ffn_kernel(*refs, n_y, final):
    x_ref, mod_ref, gf_ref = refs[0:3]
    y_refs = refs[3:3 + n_y]
    wo_refs = refs[3 + n_y:3 + 2 * n_y]
    wg_ref, wu_ref, wd_ref = refs[3 + 2 * n_y:6 + 2 * n_y]
    rest = refs[6 + 2 * n_y:]
    o_ref = rest[-1]
    mix = None
    for y_ref, wo_ref in zip(y_refs, wo_refs):
        t = _dot(y_ref[0], wo_ref[...])
        mix = t if mix is None else mix + t
    x1 = x_ref[0] + mod_ref[0, 2:3, :] * mix
    h = (_rms(x1, gf_ref[...]) * (1.0 + mod_ref[0, 4:5, :]) + mod_ref[0, 3:4, :]).astype(BF16)
    gate = _dot(h, wg_ref[...])
    up = _dot(h, wu_ref[...])
    act = (gate * jax.nn.sigmoid(gate) * up).astype(BF16)
    x2 = x1 + mod_ref[0, 5:6, :] * _dot(act, wd_ref[...])
    if final:
        x2 = _rms(x2, rest[0][...])
    o_ref[0] = x2


def _mix_ffn(x, mod, g_ffn, ys, wos, wg, wu, wd, g_final=None, tm=512):
    B, S, D = x.shape
    FF = wg.shape[1]
    final = g_final is not None
    row_blk = lambda w: pl.BlockSpec((1, tm, w), lambda b, i: (b, i, 0))
    in_specs = ([row_blk(D), pl.BlockSpec((1, 6, D), lambda b, i: (b, 0, 0)), _const_spec((1, D))]
                + [row_blk(y.shape[2]) for y in ys]
                + [_const_spec(w.shape) for w in wos]
                + [_const_spec((D, FF)), _const_spec((D, FF)), _const_spec((FF, D))])
    args = [x, mod, g_ffn.reshape(1, D), *ys, *wos, wg, wu, wd]
    if final:
        in_specs.append(_const_spec((1, D)))
        args.append(g_final.reshape(1, D))
    return pl.pallas_call(
        functools.partial(_mix_ffn_kernel, n_y=len(ys), final=final),
        grid=(B, S // tm),
        in_specs=in_specs,
        out_specs=row_blk(D),
        out_shape=jax.ShapeDtypeStruct((B, S, D), F32),
        compiler_params=_cparams(2),
        name="mix_ffn",
    )(*args)


def _block_diag(w):
    nb, bi, bo = w.shape
    eye = jnp.eye(nb, dtype=w.dtype)
    return (w[:, :, None, :] * eye[:, None, :, None]).reshape(nb * bi, nb * bo)


def _pair_heads(n_groups, per_group):
    assert n_groups == 2
    return [g * per_group + r for r in range(per_group) for g in range(n_groups)]


def _head_cols(order):
    return np.concatenate([np.arange(h * HEAD_DIM, (h + 1) * HEAD_DIM) for h in order])


def _pad_cols(w, n):
    return jnp.pad(w, ((0, 0), (0, n - w.shape[1])))


def _compress_weights(pe, w1, w2):
    G, D, H = NSA_KV_GROUPS, HEAD_DIM, CMP_HIDDEN
    half = CMP_LEN // 2
    pe_r = jnp.broadcast_to(pe.reshape(2, half, 1, D), (2, half, G, D)).reshape(2, half * G * D)
    eye = jnp.eye(G, dtype=w1.dtype)
    w1e = (w1.reshape(2, half, 1, D, 1, H) * eye[None, None, :, None, :, None]).reshape(2, half * G * D, G * H)
    w2e = (w2[None, :, None, :] * eye[:, None, :, None]).reshape(G * H, G * D)
    return pe_r, w1e.astype(BF16), w2e.astype(BF16)


def _layer_ab(x, c, norm_mix, norm_ffn, mod_w, mod_b, w_in, conv_w, conv_b, wa, ba, wx, bx, lam,
              pe_k, w1_k, w2_k, pe_v, w1_v, w2_v, w_out, wg, wu, wd):
    B, S, D = x.shape
    mod = _adaln(c, mod_w, mod_b).reshape(B, 6, D)
    C = LRU_WIDTH
    HQ = NSA_HEADS * HEAD_DIM
    GD = NSA_KV_GROUPS * HEAD_DIM
    order = _pair_heads(NSA_KV_GROUPS, NSA_HPG)
    o_q = 2 * C
    o_kv = o_q + HQ
    o_gt = o_kv + 6 * GD
    w_q = w_in[:, o_q:o_kv][:, _head_cols(order)]
    w_all = jnp.concatenate([w_in[:, :o_q], w_q, w_in[:, o_kv:o_gt], _pad_cols(w_in[:, o_gt:], LANES)],
                            axis=1).astype(BF16)
    ag, ax, q, kcmp, vcmp, kv4, gts = _inproj(
        x, mod, norm_mix, w_all,
        [(C, F32), (C, F32), (HQ, BF16), (GD, F32), (GD, F32), (4 * GD, BF16), (LANES, F32)])
    y_a = _lru(ag, ax, conv_w, conv_b, _block_diag(wa).astype(BF16), ba, _block_diag(wx).astype(BF16), bx, lam)
    NC = S // CMP_STRIDE
    kc, vc = _compress(kcmp.reshape(B, NC, CMP_STRIDE * GD), vcmp.reshape(B, NC, CMP_STRIDE * GD),
                       *_compress_weights(pe_k, w1_k, w2_k), *_compress_weights(pe_v, w1_v, w2_v))
    y_b = _nsa(q, gts, kc, vc, kv4)
    wo_a = w_out[:C].astype(BF16)
    wo_b = w_out[C:][_head_cols(order)].astype(BF16)
    return _mix_ffn(x, mod, norm_ffn, [y_a, y_b], [wo_a, wo_b],
                    wg.astype(BF16), wu.astype(BF16), wd.astype(BF16))


def _layer_c(x, c, norm_mix, norm_ffn, mod_w, mod_b, w_in, w_out, wg, wu, wd, final_norm):
    B, S, D = x.shape
    mod = _adaln(c, mod_w, mod_b).reshape(B, 6, D)
    HQ = DSA_HEADS * HEAD_DIM
    GD = DSA_KV_HEADS * HEAD_DIM
    HI = IDX_HEADS * IDX_DIM
    order = _pair_heads(DSA_KV_HEADS, DSA_HPG)
    o_k = HQ
    o_v = o_k + GD
    o_qi = o_v + GD
    o_ki = o_qi + HI
    o_wi = o_ki + IDX_DIM
    w_ki = w_in[:, o_ki:o_wi]
    w_all = jnp.concatenate([w_in[:, :HQ][:, _head_cols(order)], w_in[:, o_k:o_qi], w_in[:, o_qi:o_ki],
                             w_ki, w_ki, _pad_cols(w_in[:, o_wi:], LANES)], axis=1).astype(BF16)
    q, k, v, qi, ki, wi = _inproj(
        x, mod, norm_mix, w_all,
        [(HQ, BF16), (GD, BF16), (GD, BF16), (HI, BF16), (LANES, BF16), (LANES, F32)])
    y = _dsa(q, k, v, qi, ki, wi)
    wo = w_out[_head_cols(order)].astype(BF16)
    return _mix_ffn(x, mod, norm_ffn, [y], [wo], wg.astype(BF16), wu.astype(BF16), wd.astype(BF16),
                    g_final=final_norm)


def kernel(x, c, l0_norm_mix, l0_norm_ffn, l0_mod_w, l0_mod_b, l0_w_in, l0_conv_w, l0_conv_b, l0_lru_wa, l0_lru_ba, l0_lru_wx, l0_lru_bx, l0_lru_lambda, l0_cmp_pe_k, l0_cmp_w1_k, l0_cmp_w2_k, l0_cmp_pe_v, l0_cmp_w1_v, l0_cmp_w2_v, l0_w_out, l0_ffn_wg, l0_ffn_wu, l0_ffn_wd, l1_norm_mix, l1_norm_ffn, l1_mod_w, l1_mod_b, l1_w_in, l1_w_out, l1_ffn_wg, l1_ffn_wu, l1_ffn_wd, final_norm):
    x = _layer_ab(x, c, l0_norm_mix, l0_norm_ffn, l0_mod_w, l0_mod_b, l0_w_in, l0_conv_w, l0_conv_b,
                  l0_lru_wa, l0_lru_ba, l0_lru_wx, l0_lru_bx, l0_lru_lambda,
                  l0_cmp_pe_k, l0_cmp_w1_k, l0_cmp_w2_k, l0_cmp_pe_v, l0_cmp_w1_v, l0_cmp_w2_v,
                  l0_w_out, l0_ffn_wg, l0_ffn_wu, l0_ffn_wd)
    return _layer_c(x, c, l1_norm_mix, l1_norm_ffn, l1_mod_w, l1_mod_b, l1_w_in, l1_w_out,
                    l1_ffn_wg, l1_ffn_wu, l1_ffn_wd, final_norm)
```

```python
import functools

import numpy as np
import jax
import jax.numpy as jnp
from jax import lax
from jax.experimental import pallas as pl
from jax.experimental.pallas import tpu as pltpu

F32 = jnp.float32
BF16 = jnp.bfloat16

EPS = 1e-6
HEAD_DIM = 64
Q_BLOCK = 128
ATT_SUB = 2
NSA_SUB = 4
LRU_WIDTH = 512
LRU_BLOCKS = 8
CONV_WIDTH = 4
LRU_C = 8.0
NSA_HEADS = 8
NSA_KV_GROUPS = 2
NSA_HPG = NSA_HEADS // NSA_KV_GROUPS
CMP_LEN = 32
CMP_STRIDE = 16
CMP_HIDDEN = 128
SLC_LEN = 64
SLC_TOPN = 8
WIN = 512
FORCE_SCORE = 1e4
DSA_HEADS = 16
DSA_KV_HEADS = 2
DSA_HPG = DSA_HEADS // DSA_KV_HEADS
IDX_HEADS = 8
IDX_DIM = 64
IDX_TOPK_MAX = 256

LANES = 128
NEG = -1e30
VMEM_LIMIT = 56 * 1024 * 1024
QK_SCALE = HEAD_DIM ** -0.5 * float(np.log2(np.e))
FLT_MIN_BITS = 0x00800000
DSA_KEYS_PER_PROGRAM = 3584


def _cparams(n_grid):
    return pltpu.CompilerParams(dimension_semantics=("arbitrary",) * n_grid,
                                vmem_limit_bytes=VMEM_LIMIT)


def _const_spec(shape):
    nd = len(shape)
    return pl.BlockSpec(shape, lambda *_: (0,) * nd, pipeline_mode=pl.Buffered(1))


def _dot(a, b):
    return jnp.dot(a, b, preferred_element_type=F32)


def _dot_nt(a, b):
    return lax.dot_general(a, b, (((1,), (1,)), ((), ())), preferred_element_type=F32)


def _rms(x, g):
    return x * lax.rsqrt(jnp.mean(x * x, axis=-1, keepdims=True) + EPS) * g


def _adaln_kernel(c_ref, w_ref, b_ref, o_ref):
    c = c_ref[...]
    a = (c * jax.nn.sigmoid(c)).astype(BF16)
    o_ref[...] = _dot(a, w_ref[...].astype(BF16)) + b_ref[...]


def _adaln(c, w, b):
    B, D = c.shape
    N = w.shape[1]
    tn = N // 4
    return pl.pallas_call(
        _adaln_kernel,
        grid=(N // tn,),
        in_specs=[pl.BlockSpec((B, D), lambda j: (0, 0)),
                  pl.BlockSpec((D, tn), lambda j: (0, j)),
                  pl.BlockSpec((1, tn), lambda j: (0, j))],
        out_specs=pl.BlockSpec((B, tn), lambda j: (0, j)),
        out_shape=jax.ShapeDtypeStruct((B, N), F32),
        compiler_params=_cparams(1),
        name="adaln",
    )(c, w, b.reshape(1, N))


def _inproj_kernel(x_ref, mod_ref, g_ref, w_ref, *o_refs, splits):
    x = x_ref[0]
    h = _rms(x, g_ref[...]) * (1.0 + mod_ref[0, 1:2, :]) + mod_ref[0, 0:1, :]
    acc = _dot(h.astype(BF16), w_ref[...])
    for o_ref, (c0, wd) in zip(o_refs, splits):
        o_ref[0] = acc[:, c0:c0 + wd].astype(o_ref.dtype)


def _inproj(x, mod, g, w, outs, tm=512):
    B, S, D = x.shape
    N = w.shape[1]
    splits, c0 = [], 0
    for wd, _ in outs:
        splits.append((c0, wd))
        c0 += wd
    assert c0 == N
    return pl.pallas_call(
        functools.partial(_inproj_kernel, splits=tuple(splits)),
        grid=(B, S // tm),
        in_specs=[pl.BlockSpec((1, tm, D), lambda b, i: (b, i, 0)),
                  pl.BlockSpec((1, 6, D), lambda b, i: (b, 0, 0)),
                  _const_spec((1, D)),
                  _const_spec((D, N))],
        out_specs=[pl.BlockSpec((1, tm, wd), lambda b, i: (b, i, 0)) for wd, _ in outs],
        out_shape=[jax.ShapeDtypeStruct((B, S, wd), dt) for wd, dt in outs],
        compiler_params=_cparams(2),
        name="inproj",
    )(x, mod, g.reshape(1, D), w)


def _lru_kernel(ag_ref, ax_ref, cw_ref, cb_ref, wa_ref, ba_ref, wx_ref, bx_ref, lam_ref, o_ref,
                xpad, a_s, u_s, *, S, C, TC):
    PAD = 8
    xpad[0:PAD, :] = jnp.zeros((PAD, C), F32)
    xpad[PAD:PAD + S, :] = ax_ref[0]
    z = -lam_ref[...]
    sp = jnp.maximum(z, 0.0) + jnp.log(1.0 + jnp.exp(-jnp.abs(z)))
    for ci in range(S // TC):
        r0 = ci * TC
        xc = cb_ref[...]
        for k in range(CONV_WIDTH):
            off = PAD - (CONV_WIDTH - 1) + k + r0
            xc = xc + xpad[off:off + TC, :] * cw_ref[k:k + 1, :]
        xb = xc.astype(BF16)
        r = jax.nn.sigmoid(_dot(xb, wa_ref[...]) + ba_ref[...])
        i = jax.nn.sigmoid(_dot(xb, wx_ref[...]) + bx_ref[...])
        log_a = -LRU_C * r * sp
        a = jnp.exp(log_a)
        mult = jnp.sqrt(1.0 - a * a)
        if ci == 0:
            row = lax.broadcasted_iota(jnp.int32, (TC, C), 0)
            mult = jnp.where(row == 0, 1.0, mult)
        a_s[r0:r0 + TC, :] = a
        u_s[r0:r0 + TC, :] = mult * (i * xc)

    row8 = lax.broadcasted_iota(jnp.int32, (8, C), 0)

    def tile(ti, h):
        t0 = pl.multiple_of(ti * 8, 8)
        A = a_s[pl.ds(t0, 8), :]
        U = u_s[pl.ds(t0, 8), :]
        for d in (1, 2, 4):
            As = pltpu.roll(A, d, 0)
            Us = pltpu.roll(U, d, 0)
            ok = row8 >= d
            U = jnp.where(ok, A * Us + U, U)
            A = jnp.where(ok, A * As, A)
        H = U + A * h
        u_s[pl.ds(t0, 8), :] = H
        return H[7:8, :]

    lax.fori_loop(0, S // 8, tile, jnp.zeros((1, C), F32))

    for ci in range(S // TC):
        r0 = ci * TC
        o_ref[0, r0:r0 + TC, :] = (u_s[r0:r0 + TC, :]
                                   * jax.nn.gelu(ag_ref[0, r0:r0 + TC, :])).astype(o_ref.dtype)


def _lru(ag, ax, cw, cb, wa_bd, ba, wx_bd, bx, lam):
    B, S, C = ag.shape
    TC = 256
    row = lambda v: v.reshape(1, C)
    return pl.pallas_call(
        functools.partial(_lru_kernel, S=S, C=C, TC=TC),
        grid=(B,),
        in_specs=[pl.BlockSpec((1, S, C), lambda b: (b, 0, 0)),
                  pl.BlockSpec((1, S, C), lambda b: (b, 0, 0)),
                  _const_spec((CONV_WIDTH, C)), _const_spec((1, C)),
                  _const_spec((C, C)), _const_spec((1, C)),
                  _const_spec((C, C)), _const_spec((1, C)), _const_spec((1, C))],
        out_specs=pl.BlockSpec((1, S, C), lambda b: (b, 0, 0)),
        out_shape=jax.ShapeDtypeStruct((B, S, C), BF16),
        scratch_shapes=[pltpu.VMEM((S + 8, C), F32), pltpu.VMEM((S, C), F32), pltpu.VMEM((S, C), F32)],
        compiler_params=_cparams(1),
        name="rglru",
    )(ag, ax, cw, row(cb), wa_bd, row(ba), wx_bd, row(bx), row(lam))


def _compress_kernel(xk_ref, xv_ref, pek_ref, w1k_ref, w2k_ref, pev_ref, w1v_ref, w2v_ref,
                     ok_ref, ov_ref):
    def one(x_ref, pe_ref, w1_ref, w2_ref, o_ref):
        x = x_ref[0]
        p0 = _dot((x + pe_ref[0:1, :]).astype(BF16), w1_ref[0])
        p1 = _dot((x + pe_ref[1:2, :]).astype(BF16), w1_ref[1])
        nc = x.shape[0]
        hid = p0 + pltpu.roll(p1, nc - 1, 0)
        o_ref[0] = _dot(jax.nn.gelu(hid).astype(BF16), w2_ref[...]).astype(o_ref.dtype)

    one(xk_ref, pek_ref, w1k_ref, w2k_ref, ok_ref)
    one(xv_ref, pev_ref, w1v_ref, w2v_ref, ov_ref)


def _compress(xk, xv, pek, w1k, w2k, pev, w1v, w2v):
    B, NC, W = xk.shape
    GH = NSA_KV_GROUPS * CMP_HIDDEN
    GD = NSA_KV_GROUPS * HEAD_DIM
    xspec = pl.BlockSpec((1, NC, W), lambda b: (b, 0, 0))
    ospec = pl.BlockSpec((1, NC, GD), lambda b: (b, 0, 0))
    return pl.pallas_call(
        _compress_kernel,
        grid=(B,),
        in_specs=[xspec, xspec,
                  _const_spec((2, W)), _const_spec((2, W, GH)), _const_spec((GH, GD)),
                  _const_spec((2, W)), _const_spec((2, W, GH)), _const_spec((GH, GD))],
        out_specs=[ospec, ospec],
        out_shape=[jax.ShapeDtypeStruct((B, NC, GD), BF16)] * 2,
        compiler_params=_cparams(1),
        name="nsa_compress",
    )(xk, xv, pek, w1k, w2k, pev, w1v, w2v)


def _causal_buckets(S, step):
    step = min(S, step)
    assert S % step == 0
    return tuple(range(step, S + 1, step))


def _for_bucket(needed, buckets, body):
    lo = 0
    for lk in buckets:
        pl.when((needed > lo) & (needed <= lk))(functools.partial(body, lk))
        lo = lk


def _eye(QB):
    return (lax.broadcasted_iota(jnp.int32, (QB, LANES), 0)
            == lax.broadcasted_iota(jnp.int32, (QB, LANES), 1)).astype(BF16)


def _q_rows(slot, keep, eye):
    return jnp.concatenate([jnp.where(keep, slot, jnp.zeros((), BF16)), eye], axis=1)


def _colsum(x):
    n, q = x.shape
    part = jnp.sum(x.reshape(n // 64, 64, q), axis=0)
    return jnp.sum(part, axis=0, keepdims=True)


def _colreduce(x, op):
    n, q = x.shape
    return op(op(x.reshape(n // 64, 64, q), axis=0), axis=0, keepdims=True)


def _f32_to_rank(x):
    bits = pltpu.bitcast(x, jnp.int32)
    key = bits ^ (lax.shift_right_arithmetic(bits, 31) & 0x7FFFFFFF)
    return jnp.where(key >= FLT_MIN_BITS, key - (FLT_MIN_BITS - 1), jnp.where(key < -FLT_MIN_BITS, key + FLT_MIN_BITS, 0))


def _rank_to_f32(r):
    key = jnp.where(r > 0, r + (FLT_MIN_BITS - 1), jnp.where(r < 0, r - FLT_MIN_BITS, 0))
    return pltpu.bitcast(key ^ (lax.shift_right_arithmetic(key, 31) & 0x7FFFFFFF), F32)


def _kth_largest(x, x_valid_min, n_valid, k):
    few = n_valid <= k
    lo0 = _f32_to_rank(_colreduce(x_valid_min, jnp.min))
    hi0 = _f32_to_rank(_colreduce(x, jnp.max)) + 1
    done0 = jnp.where(few | (lo0 + 1 == hi0), 1, 0).astype(jnp.int32)
    FIRST_STEPS, CHECK_EVERY, VALUE_STEPS, MAX_STEPS = 20, 4, 24, 64

    def step(u, state):
        it, lo, hi, lof, hif, done = state
        mid_f = 0.5 * lof + 0.5 * hif
        mid_v = _f32_to_rank(mid_f)
        mid_k = lax.shift_right_arithmetic(lo, 1) + lax.shift_right_arithmetic(hi, 1) + (lo & hi & 1)
        use_v = (mid_v > lo) & (mid_v < hi) & (it < VALUE_STEPS)
        near0 = jnp.where((lo < 0) & (hi > 0), 0, jnp.where(lo == 0, 1, -1))
        zero_step = ((lo < 0) & (hi > 0)) | (lo == 0) | (hi == 0)
        mid_k = jnp.where(zero_step, near0, mid_k)
        use_v = use_v & jnp.logical_not(zero_step)
        mid = jnp.where(use_v, mid_v, mid_k)
        midf = jnp.where(use_v, mid_f, _rank_to_f32(mid_k))
        cnt = _colsum(jnp.where(x >= midf, 1.0, 0.0))
        up = (done == 0) & (cnt >= k)
        down = (done == 0) & (cnt < k)
        lo, lof = jnp.where(up, mid, lo), jnp.where(up, midf, lof)
        hi, hif = jnp.where(down, mid, hi), jnp.where(down, midf, hif)
        done = jnp.where((cnt == k) | (lo + 1 == hi), 1, done)
        return it + 1, lo, hi, lof, hif, done

    def body(carry):
        state = lax.fori_loop(0, CHECK_EVERY, step, carry[0:6])
        return (*state, jnp.min(state[5].astype(F32)))

    def cond(carry):
        return (carry[0] < MAX_STEPS) & (carry[6] < 0.5)

    state = lax.fori_loop(0, FIRST_STEPS, step, (jnp.int32(0), lo0, hi0, _rank_to_f32(lo0), _rank_to_f32(hi0), done0))
    lof = lax.while_loop(cond, body, (*state, jnp.min(state[5].astype(F32))))[3]
    return jnp.where(few, -jnp.inf, lof)


def _attend(qa, kaug, vaug):
    s = _dot_nt(qa, kaug)
    m = jnp.max(s, axis=-1, keepdims=True)
    oa = _dot(jnp.exp2(s - m).astype(BF16), vaug)
    return oa[:, 0:LANES], oa[:, LANES:2 * LANES]


def _nsa_body(LK, sub, q_ref, gt_ref, kc_ref, vc_ref, kv_ref, ovt_ref, expt_ref, o_ref, *, QB, NC, NSEL, NTOP):
    t0 = LK - (NSA_SUB - sub) * QB
    rows = slice(sub * QB, (sub + 1) * QB)
    R = NSA_HPG
    G = NSA_KV_GROUPS
    low = lax.broadcasted_iota(jnp.int32, (QB, LANES), 1) < HEAD_DIM
    keep_g = (low, jnp.logical_not(low))
    eye = _eye(QB)
    q = (q_ref[0, rows, :].astype(F32) * QK_SCALE).astype(BF16)
    gates = jax.nn.sigmoid(gt_ref[0, rows, :])
    qa_g = [jnp.concatenate([_q_rows(q[:, r * LANES:(r + 1) * LANES], keep_g[g], eye) for r in range(R)], axis=0)
            for g in range(G)]
    qa = jnp.concatenate(qa_g, axis=0)

    def tq_t(n):
        return t0 + lax.broadcasted_iota(jnp.int32, (n, QB), 1)

    def vaug(v):
        return jnp.concatenate([v, jnp.ones(v.shape, BF16)], axis=1)

    cend = lax.broadcasted_iota(jnp.int32, (NC, QB), 0) * CMP_STRIDE + (CMP_LEN - 1)
    bias_c = jnp.where(cend <= tq_t(NC), 0.0, NEG).astype(BF16)
    s = _dot_nt(qa, jnp.concatenate([kc_ref[0], bias_c], axis=1))
    m = jnp.max(s, axis=-1, keepdims=True)
    p = jnp.exp2(s - m)
    inv = jnp.where(m > 0.5 * NEG, 1.0 / jnp.maximum(jnp.sum(p, axis=-1, keepdims=True), 1e-30), 0.0)
    o_cmp = _dot(p.astype(BF16), vc_ref[0]) * inv
    pn = p * inv

    start = max(t0 - WIN, 0)
    nwin = t0 + QB - start
    kpos_w = start + lax.broadcasted_iota(jnp.int32, (nwin, QB), 0)
    bias_w = jnp.where((kpos_w <= tq_t(nwin)) & (kpos_w > tq_t(nwin) - WIN), 0.0, NEG).astype(BF16)
    kwin = kv_ref[0, start:start + nwin, 2 * LANES:3 * LANES]
    vwin = kv_ref[0, start:start + nwin, 3 * LANES:4 * LANES]
    ow, lw = _attend(qa, jnp.concatenate([kwin, bias_w], axis=1), vaug(vwin))
    o_win = ow / lw

    jrow = lax.broadcasted_iota(jnp.int32, (LANES, QB), 0)
    cur = lax.shift_right_logical(tq_t(LANES), 6)
    forced = (jrow == 0) | (jrow == cur) | (jrow == cur - 1)
    LS = t0 + QB
    causal_s = lax.broadcasted_iota(jnp.int32, (LS, QB), 0) <= tq_t(LS)
    kslc = kv_ref[0, 0:LS, 0:LANES]
    vslc = vaug(kv_ref[0, 0:LS, LANES:2 * LANES])
    o_slc = []
    for g in range(G):
        base = g * R * QB
        psum = pn[base:base + QB]
        for r in range(1, R):
            psum = psum + pn[base + r * QB:base + (r + 1) * QB]
        hi = psum.astype(BF16)
        lo = (psum - hi.astype(F32)).astype(BF16)
        imp = _dot_nt(ovt_ref[...], hi) + _dot_nt(ovt_ref[...], lo)
        imp = jnp.where(forced, FORCE_SCORE, imp)
        imp = jnp.where(jrow <= cur, imp, -jnp.inf)
        imp = imp[0:NSEL]
        jr = jrow[0:NSEL]
        cnt = jnp.zeros((NSEL, QB), F32)
        for j2 in range(NSEL):
            rowv = imp[j2:j2 + 1, :]
            cnt = cnt + jnp.where(jr > j2, jnp.where(rowv >= imp, 1.0, 0.0), jnp.where(rowv > imp, 1.0, 0.0))
        sel_t = jnp.where(cnt < NTOP, 1.0, 0.0)
        if NSEL < LANES:
            sel_t = jnp.concatenate([sel_t, jnp.zeros((LANES - NSEL, QB), F32)], axis=0)
        picked = _dot(expt_ref[0:LS, :], sel_t.astype(BF16))
        bias_s = jnp.where((picked > 0.5) & causal_s, 0.0, NEG).astype(BF16)
        os_, ls = _attend(qa_g[g], jnp.concatenate([kslc, bias_s], axis=1), vslc)
        o_slc.append(os_ / ls)

    for r in range(R):
        acc = jnp.zeros((QB, LANES), F32)
        ra, rb = r * QB, (R + r) * QB
        branches = ((o_cmp[ra:ra + QB], o_cmp[rb:rb + QB]),
                    (o_slc[0][ra:ra + QB], o_slc[1][ra:ra + QB]),
                    (o_win[ra:ra + QB], o_win[rb:rb + QB]))
        for j, (va, vb) in enumerate(branches):
            ca, cb = 3 * r + j, 3 * (R + r) + j
            gate = jnp.where(low, gates[:, ca:ca + 1], gates[:, cb:cb + 1])
            acc = acc + gate * jnp.where(low, va, vb)
        o_ref[0, rows, r * LANES:(r + 1) * LANES] = acc.astype(o_ref.dtype)


def _nsa_kernel(*refs, LK, **kw):
    for sub in range(NSA_SUB):
        _nsa_body(LK, sub, *refs, **kw)


def _nsa(q, gts, kc, vc, kv4):
    B, S, _ = q.shape
    QB = Q_BLOCK
    NC = kc.shape[1]
    NSEL = S // SLC_LEN
    NTOP = min(SLC_TOPN, NSEL)
    TQ = NSA_SUB * QB
    assert NSEL <= LANES and S % TQ == 0 and S >= WIN + QB
    c = np.arange(NC)[None, :] * CMP_STRIDE
    j = np.arange(LANES)[:, None] * SLC_LEN
    valid_c = np.arange(NC)[None, :] < (S - CMP_LEN) // CMP_STRIDE + 1
    ovt = ((c < j + SLC_LEN) & (c + CMP_LEN > j) & valid_c & (np.arange(LANES)[:, None] < NSEL))
    expand_t = (np.arange(S)[:, None] // SLC_LEN) == np.arange(LANES)[None, :]
    GD = NSA_KV_GROUPS * HEAD_DIM
    ovt, expand_t = jnp.asarray(ovt, BF16), jnp.asarray(expand_t, BF16)
    outs = []
    for step in range(S // TQ):
        blk = lambda w, s=step: pl.BlockSpec((1, TQ, w), lambda b: (b, s, 0))
        outs.append(pl.pallas_call(
            functools.partial(_nsa_kernel, LK=(step + 1) * TQ, QB=QB, NC=NC, NSEL=NSEL, NTOP=NTOP),
            grid=(B,),
            in_specs=[blk(NSA_HPG * LANES), blk(LANES),
                      pl.BlockSpec((1, NC, GD), lambda b: (b, 0, 0)),
                      pl.BlockSpec((1, NC, GD), lambda b: (b, 0, 0)),
                      pl.BlockSpec((1, S, 4 * GD), lambda b: (b, 0, 0)),
                      _const_spec((LANES, NC)), _const_spec((S, LANES))],
            out_specs=pl.BlockSpec((1, TQ, NSA_HPG * LANES), lambda b: (b, 0, 0)),
            out_shape=jax.ShapeDtypeStruct((B, TQ, NSA_HPG * LANES), BF16),
            compiler_params=_cparams(1),
            name="nsa_attention",
        )(q, gts, kc, vc, kv4, ovt, expand_t))
    return jnp.concatenate(outs, axis=1)


def _dsa_body(LK, q_ref, k_ref, v_ref, qi_ref, ki_ref, wi_ref, o_ref, *, QB, NKEEP, CH, step0):
    TQ = ATT_SUB * QB
    t0 = (pl.program_id(0) + step0) * TQ
    low_t = lax.broadcasted_iota(jnp.int32, (TQ, LANES), 1) < HEAD_DIM
    zero = jnp.zeros((), BF16)
    causal = (lax.broadcasted_iota(jnp.int32, (LK, TQ), 0)
              <= t0 + lax.broadcasted_iota(jnp.int32, (LK, TQ), 1))

    w_t = (wi_ref[0] * IDX_DIM ** -0.5).T
    qi = qi_ref[0]
    ki = ki_ref[0, 0:LK, :]
    score = None
    for u in range(IDX_HEADS // 2):
        slot = qi[:, u * LANES:(u + 1) * LANES]
        pair = jnp.concatenate([jnp.where(low_t, slot, zero), jnp.where(low_t, zero, slot)], axis=0)
        sc = jnp.maximum(_dot_nt(ki, pair), 0.0)
        term = sc[:, 0:TQ] * w_t[2 * u:2 * u + 1, :] + sc[:, TQ:2 * TQ] * w_t[2 * u + 1:2 * u + 2, :]
        score = term if score is None else score + term

    score = score * IDX_HEADS ** -0.5
    masked = jnp.where(causal, score, -jnp.inf)
    n_valid = t0 + lax.broadcasted_iota(jnp.int32, (1, TQ), 1) + 1
    thr = _kth_largest(masked, jnp.where(causal, score, jnp.inf), n_valid, NKEEP)
    gt = masked > thr
    eq = masked == thr
    need = NKEEP - _colsum(jnp.where(gt, 1.0, 0.0))
    eq_b = jnp.where(eq, 1.0, 0.0).astype(BF16)
    tri = (lax.broadcasted_iota(jnp.int32, (CH, CH), 0)
           > lax.broadcasted_iota(jnp.int32, (CH, CH), 1)).astype(BF16)
    ranks, before = [], jnp.zeros((1, TQ), F32)
    for c in range(LK // CH):
        e = eq_b[c * CH:(c + 1) * CH]
        ranks.append(_dot(tri, e) + before)
        before = before + _colsum(e.astype(F32))
    rank = jnp.concatenate(ranks, axis=0)
    bias = jnp.where((gt | (eq & (rank < need))) & causal, 0.0, NEG).astype(BF16)
    k = k_ref[0, 0:LK, :]
    vaug = jnp.concatenate([v_ref[0, 0:LK, :], jnp.ones((LK, LANES), BF16)], axis=1)

    low = lax.broadcasted_iota(jnp.int32, (QB, LANES), 1) < HEAD_DIM
    not_low = jnp.logical_not(low)
    eye = _eye(QB)
    HALF = DSA_HPG // 2
    for sub in range(ATT_SUB):
        rows_q = slice(sub * QB, (sub + 1) * QB)
        ls = LK - (ATT_SUB - 1 - sub) * QB
        kaug = jnp.concatenate([k[0:ls], bias[0:ls, sub * QB:(sub + 1) * QB]], axis=1)
        for half in range(2):
            rows = []
            for r in range(half * HALF, (half + 1) * HALF):
                slot = (q_ref[0, rows_q, r * LANES:(r + 1) * LANES].astype(F32) * QK_SCALE).astype(BF16)
                rows += [_q_rows(slot, low, eye), _q_rows(slot, not_low, eye)]
            o, l = _attend(jnp.concatenate(rows, axis=0), kaug, vaug[0:ls])
            o = o / l
            for i in range(HALF):
                r = half * HALF + i
                o_ref[0, rows_q, r * LANES:(r + 1) * LANES] = jnp.where(
                    low, o[2 * i * QB:(2 * i + 1) * QB], o[(2 * i + 1) * QB:(2 * i + 2) * QB]).astype(o_ref.dtype)


def _dsa_kernel(*refs, QB, buckets, step0, **kw):
    needed = (pl.program_id(0) + step0 + 1) * ATT_SUB * QB
    _for_bucket(needed, buckets, lambda lk: _dsa_body(lk, *refs, QB=QB, step0=step0, **kw))


def _bucket_groups(lengths):
    groups, cur = [], []
    for lk in lengths:
        if cur and sum(cur) + lk > DSA_KEYS_PER_PROGRAM:
            groups.append(tuple(cur))
            cur = []
        cur.append(lk)
    groups.append(tuple(cur))
    return groups


def _dsa(q, k, v, qi, ki, wi):
    B, S, _ = q.shape
    QB = Q_BLOCK
    NKEEP = min(IDX_TOPK_MAX, S // 4)
    GD = DSA_KV_HEADS * HEAD_DIM
    TQ = ATT_SUB * QB
    assert S % TQ == 0
    full = lambda w: pl.BlockSpec((1, S, w), lambda i, b: (b, 0, 0))
    outs, step0 = [], 0
    for buckets in _bucket_groups(_causal_buckets(S, TQ)):
        n = len(buckets)
        blk = lambda w, s0=step0: pl.BlockSpec((1, TQ, w), lambda i, b: (b, i + s0, 0))
        outs.append(pl.pallas_call(
            functools.partial(_dsa_kernel, QB=QB, NKEEP=NKEEP, CH=256, buckets=buckets, step0=step0),
            grid=(n, B),
            in_specs=[blk(DSA_HPG * LANES), full(GD), full(GD), blk(IDX_HEADS * IDX_DIM), full(LANES), blk(LANES)],
            out_specs=pl.BlockSpec((1, TQ, DSA_HPG * LANES), lambda i, b: (b, i, 0)),
            out_shape=jax.ShapeDtypeStruct((B, n * TQ, DSA_HPG * LANES), BF16),
            compiler_params=_cparams(2),
            name="dsa_attention",
        )(q, k, v, qi, ki, wi))
        step0 += n
    return jnp.concatenate(outs, axis=1)


def _mix_ffn_kernel(*refs, n_y, final):
    x_ref, mod_ref, gf_ref = refs[0:3]
    y_refs = refs[3:3 + n_y]
    wo_refs = refs[3 + n_y:3 + 2 * n_y]
    wg_ref, wu_ref, wd_ref = refs[3 + 2 * n_y:6 + 2 * n_y]
    rest = refs[6 + 2 * n_y:]
    o_ref = rest[-1]
    mix = None
    for y_ref, wo_ref in zip(y_refs, wo_refs):
        t = _dot(y_ref[0], wo_ref[...])
        mix = t if mix is None else mix + t
    x1 = x_ref[0] + mod_ref[0, 2:3, :] * mix
    h = (_rms(x1, gf_ref[...]) * (1.0 + mod_ref[0, 4:5, :]) + mod_ref[0, 3:4, :]).astype(BF16)
    gate = _dot(h, wg_ref[...])
    up = _dot(h, wu_ref[...])
    act = (gate * jax.nn.sigmoid(gate) * up).astype(BF16)
    x2 = x1 + mod_ref[0, 5:6, :] * _dot(act, wd_ref[...])
    if final:
        x2 = _rms(x2, rest[0][...])
    o_ref[0] = x2


def _mix_ffn(x, mod, g_ffn, ys, wos, wg, wu, wd, g_final=None, tm=512):
    B, S, D = x.shape
    FF = wg.shape[1]
    final = g_final is not None
    row_blk = lambda w: pl.BlockSpec((1, tm, w), lambda b, i: (b, i, 0))
    in_specs = ([row_blk(D), pl.BlockSpec((1, 6, D), lambda b, i: (b, 0, 0)), _const_spec((1, D))]
                + [row_blk(y.shape[2]) for y in ys]
                + [_const_spec(w.shape) for w in wos]
                + [_const_spec((D, FF)), _const_spec((D, FF)), _const_spec((FF, D))])
    args = [x, mod, g_ffn.reshape(1, D), *ys, *wos, wg, wu, wd]
    if final:
        in_specs.append(_const_spec((1, D)))
        args.append(g_final.reshape(1, D))
    return pl.pallas_call(
        functools.partial(_mix_ffn_kernel, n_y=len(ys), final=final),
        grid=(B, S // tm),
        in_specs=in_specs,
        out_specs=row_blk(D),
        out_shape=jax.ShapeDtypeStruct((B, S, D), F32),
        compiler_params=_cparams(2),
        name="mix_ffn",
    )(*args)


def _block_diag(w):
    nb, bi, bo = w.shape
    eye = jnp.eye(nb, dtype=w.dtype)
    return (w[:, :, None, :] * eye[:, None, :, None]).reshape(nb * bi, nb * bo)


def _pair_heads(n_groups, per_group):
    assert n_groups == 2
    return [g * per_group + r for r in range(per_group) for g in range(n_groups)]


def _head_cols(order):
    return np.concatenate([np.arange(h * HEAD_DIM, (h + 1) * HEAD_DIM) for h in order])


def _pad_cols(w, n):
    return jnp.pad(w, ((0, 0), (0, n - w.shape[1])))


def _compress_weights(pe, w1, w2):
    G, D, H = NSA_KV_GROUPS, HEAD_DIM, CMP_HIDDEN
    half = CMP_LEN // 2
    pe_r = jnp.broadcast_to(pe.reshape(2, half, 1, D), (2, half, G, D)).reshape(2, half * G * D)
    eye = jnp.eye(G, dtype=w1.dtype)
    w1e = (w1.reshape(2, half, 1, D, 1, H) * eye[None, None, :, None, :, None]).reshape(2, half * G * D, G * H)
    w2e = (w2[None, :, None, :] * eye[:, None, :, None]).reshape(G * H, G * D)
    return pe_r, w1e.astype(BF16), w2e.astype(BF16)


def _layer_ab(x, c, norm_mix, norm_ffn, mod_w, mod_b, w_in, conv_w, conv_b, wa, ba, wx, bx, lam,
              pe_k, w1_k, w2_k, pe_v, w1_v, w2_v, w_out, wg, wu, wd):
    B, S, D = x.shape
    mod = _adaln(c, mod_w, mod_b).reshape(B, 6, D)
    C = LRU_WIDTH
    HQ = NSA_HEADS * HEAD_DIM
    GD = NSA_KV_GROUPS * HEAD_DIM
    order = _pair_heads(NSA_KV_GROUPS, NSA_HPG)
    o_q = 2 * C
    o_kv = o_q + HQ
    o_gt = o_kv + 6 * GD
    w_q = w_in[:, o_q:o_kv][:, _head_cols(order)]
    w_all = jnp.concatenate([w_in[:, :o_q], w_q, w_in[:, o_kv:o_gt], _pad_cols(w_in[:, o_gt:], LANES)],
                            axis=1).astype(BF16)
    ag, ax, q, kcmp, vcmp, kv4, gts = _inproj(
        x, mod, norm_mix, w_all,
        [(C, F32), (C, F32), (HQ, BF16), (GD, F32), (GD, F32), (4 * GD, BF16), (LANES, F32)])
    y_a = _lru(ag, ax, conv_w, conv_b, _block_diag(wa).astype(BF16), ba, _block_diag(wx).astype(BF16), bx, lam)
    NC = S // CMP_STRIDE
    kc, vc = _compress(kcmp.reshape(B, NC, CMP_STRIDE * GD), vcmp.reshape(B, NC, CMP_STRIDE * GD),
                       *_compress_weights(pe_k, w1_k, w2_k), *_compress_weights(pe_v, w1_v, w2_v))
    y_b = _nsa(q, gts, kc, vc, kv4)
    wo_a = w_out[:C].astype(BF16)
    wo_b = w_out[C:][_head_cols(order)].astype(BF16)
    return _mix_ffn(x, mod, norm_ffn, [y_a, y_b], [wo_a, wo_b],
                    wg.astype(BF16), wu.astype(BF16), wd.astype(BF16))


def _layer_c(x, c, norm_mix, norm_ffn, mod_w, mod_b, w_in, w_out, wg, wu, wd, final_norm):
    B, S, D = x.shape
    mod = _adaln(c, mod_w, mod_b).reshape(B, 6, D)
    HQ = DSA_HEADS * HEAD_DIM
    GD = DSA_KV_HEADS * HEAD_DIM
    HI = IDX_HEADS * IDX_DIM
    order = _pair_heads(DSA_KV_HEADS, DSA_HPG)
    o_k = HQ
    o_v = o_k + GD
    o_qi = o_v + GD
    o_ki = o_qi + HI
    o_wi = o_ki + IDX_DIM
    w_ki = w_in[:, o_ki:o_wi]
    w_all = jnp.concatenate([w_in[:, :HQ][:, _head_cols(order)], w_in[:, o_k:o_qi], w_in[:, o_qi:o_ki],
                             w_ki, w_ki, _pad_cols(w_in[:, o_wi:], LANES)], axis=1).astype(BF16)
    q, k, v, qi, ki, wi = _inproj(
        x, mod, norm_mix, w_all,
        [(HQ, BF16), (GD, BF16), (GD, BF16), (HI, BF16), (LANES, BF16), (LANES, F32)])
    y = _dsa(q, k, v, qi, ki, wi)
    wo = w_out[_head_cols(order)].astype(BF16)
    return _mix_ffn(x, mod, norm_ffn, [y], [wo], wg.astype(BF16), wu.astype(BF16), wd.astype(BF16),
                    g_final=final_norm)


def kernel(x, c, l0_norm_mix, l0_norm_ffn, l0_mod_w, l0_mod_b, l0_w_in, l0_conv_w, l0_conv_b, l0_lru_wa, l0_lru_ba, l0_lru_wx, l0_lru_bx, l0_lru_lambda, l0_cmp_pe_k, l0_cmp_w1_k, l0_cmp_w2_k, l0_cmp_pe_v, l0_cmp_w1_v, l0_cmp_w2_v, l0_w_out, l0_ffn_wg, l0_ffn_wu, l0_ffn_wd, l1_norm_mix, l1_norm_ffn, l1_mod_w, l1_mod_b, l1_w_in, l1_w_out, l1_ffn_wg, l1_ffn_wu, l1_ffn_wd, final_norm):
    x = _layer_ab(x, c, l0_norm_mix, l0_norm_ffn, l0_mod_w, l0_mod_b, l0_w_in, l0_conv_w, l0_conv_b,
                  l0_lru_wa, l0_lru_ba, l0_lru_wx, l0_lru_bx, l0_lru_lambda,
                  l0_cmp_pe_k, l0_cmp_w1_k, l0_cmp_w2_k, l0_cmp_pe_v, l0_cmp_w1_v, l0_cmp_w2_v,
                  l0_w_out, l0_ffn_wg, l0_ffn_wu, l0_ffn_wd)
    return _layer_c(x, c, l1_norm_mix, l1_norm_ffn, l1_mod_w, l1_mod_b, l1_w_in, l1_w_out,
                    l1_ffn_wg, l1_ffn_wu, l1_ffn_wd, final_norm)
```
